```python
import jax, jax.numpy as jnp
from jax import lax
import numpy as np

D_MODEL = 2048
BATCH = 1
SEQ = 16384
DEPTH = 2

CHUNK = 128
SSM_D_INNER = 2 * D_MODEL
SSM_HEAD_DIM = 64
SSM_HEADS = SSM_D_INNER // SSM_HEAD_DIM
SSM_GROUPS = 8
SSM_HPG = SSM_HEADS // SSM_GROUPS
SSM_STATE = 128
CONV_WIDTH = 4
SSM_XBC = SSM_D_INNER + 2 * SSM_GROUPS * SSM_STATE
RET_HEADS = 8
RET_QK_DIM = D_MODEL // RET_HEADS
RET_V_DIM = 2 * RET_QK_DIM
RET_QK_WIDTH = RET_HEADS * RET_QK_DIM
RET_V_WIDTH = RET_HEADS * RET_V_DIM
ROPE_BASE = 10000.0
N_BRANCHES = 2
PROJ_SIZES = (SSM_D_INNER, SSM_XBC, SSM_HEADS, RET_QK_WIDTH, RET_QK_WIDTH, RET_V_WIDTH, RET_V_WIDTH, N_BRANCHES * D_MODEL)
PROJ_WIDTH = sum(PROJ_SIZES)
PEER_HEADS = 8
PEER_N_KEYS = 128
PEER_EXPERTS = PEER_N_KEYS * PEER_N_KEYS
PEER_TOPK = 16
PEER_QUERY_DIM = 256
PEER_HALF = PEER_QUERY_DIM // 2
PEER_TOKEN_BLOCK = 128
NORM_EPS = 1e-6

kernel_name = "hybrid_ssd_retention_peer"


def rms_norm(x, w=None):
    xf = x.astype(jnp.float32)
    y = xf * lax.rsqrt(jnp.mean(xf * xf, axis=-1, keepdims=True) + NORM_EPS)
    if w is not None:
        y = y * w.astype(jnp.float32)
    return y.astype(x.dtype)


def rotary(x, positions):
    half = x.shape[-1] // 2
    inv_freq = 1.0 / (ROPE_BASE ** (jnp.arange(half, dtype=jnp.float32) / half))
    ang = positions.astype(jnp.float32)[..., None] * inv_freq
    cos = jnp.cos(ang)[:, :, None, :]
    sin = jnp.sin(ang)[:, :, None, :]
    x1 = x[..., :half].astype(jnp.float32)
    x2 = x[..., half:].astype(jnp.float32)
    return jnp.concatenate([x1 * cos - x2 * sin, x2 * cos + x1 * sin], axis=-1).astype(x.dtype)


def causal_depthwise_conv(x, w, bias):
    c = x.shape[-1]
    y = lax.conv_general_dilated(x, w[:, None, :].astype(x.dtype), window_strides=(1,), padding=[(CONV_WIDTH - 1, 0)], dimension_numbers=('NWC', 'WIO', 'NWC'), feature_group_count=c)
    return y + bias


def chunked_decay_scan(q, k, v, log_a):
    b, t, g, n = q.shape
    r, p = v.shape[3], v.shape[4]
    c = t // CHUNK
    q = q.reshape(b, c, CHUNK, g, n)
    k = k.reshape(b, c, CHUNK, g, n)
    v = v.reshape(b, c, CHUNK, g, r, p)
    cum = jnp.cumsum(log_a.astype(jnp.float32).reshape(b, c, CHUNK, g, r), axis=2)
    mask = jnp.tril(jnp.ones((CHUNK, CHUNK), dtype=bool))[None, None, :, :, None, None]
    seg = cum[:, :, :, None] - cum[:, :, None, :]
    decay = jnp.exp(jnp.where(mask, seg, -jnp.inf))
    scores = jnp.einsum('bclgn,bcsgn->bclsg', q, k)
    y_intra = jnp.einsum('bclsgr,bcsgrp->bclgrp', scores[..., None] * decay, v)
    total = cum[:, :, -1]
    v_dec = v * jnp.exp(total[:, :, None] - cum)[..., None]
    states = jnp.einsum('bcsgn,bcsgrp->bcgrnp', k, v_dec)

    def step(h, inp):
        st, tot = inp
        h_new = jnp.exp(tot)[..., None, None] * h + st
        return h_new, h

    h0 = jnp.zeros_like(states[:, 0])
    _, prev = lax.scan(step, h0, (jnp.moveaxis(states, 1, 0), jnp.moveaxis(total, 1, 0)))
    prev = jnp.moveaxis(prev, 0, 1)
    y_inter = jnp.einsum('bclgn,bcgrnp->bclgrp', q, prev) * jnp.exp(cum)[..., None]
    return (y_intra + y_inter).reshape(b, t, g, r, p)


def hybrid_mixer(h, positions, w_in, conv_w, conv_b, dt_bias, a_log, d_skip, ssm_norm, w_ssm_out, w_ret_out, b_gate, w_o):
    b, t, _ = h.shape
    points = np.cumsum(PROJ_SIZES)[:-1].tolist()
    proj = h @ w_in
    z, xbc, dt_raw, rq, rk, rv, rg, gate_pre = jnp.split(proj, points, axis=-1)

    xbc = jax.nn.silu(causal_depthwise_conv(xbc, conv_w, conv_b))
    xs, bmat, cmat = jnp.split(xbc, [SSM_D_INNER, SSM_D_INNER + SSM_GROUPS * SSM_STATE], axis=-1)
    dt = jax.nn.softplus((dt_raw + dt_bias).astype(jnp.float32)).reshape(b, t, SSM_GROUPS, SSM_HPG)
    a = -jnp.exp(a_log.astype(jnp.float32)).reshape(SSM_GROUPS, SSM_HPG)
    xs = xs.reshape(b, t, SSM_GROUPS, SSM_HPG, SSM_HEAD_DIM)
    y = chunked_decay_scan(cmat.reshape(b, t, SSM_GROUPS, SSM_STATE), bmat.reshape(b, t, SSM_GROUPS, SSM_STATE), xs * dt[..., None], dt * a)
    y = y + xs * d_skip.reshape(SSM_GROUPS, SSM_HPG)[:, :, None]
    y = y.reshape(b, t, SSM_D_INNER) * jax.nn.silu(z)
    y = rms_norm(y.reshape(b, t, SSM_GROUPS, SSM_D_INNER // SSM_GROUPS), ssm_norm.reshape(SSM_GROUPS, -1)).reshape(b, t, SSM_D_INNER)
    y_ssm = y @ w_ssm_out

    rq = rotary(rq.reshape(b, t, RET_HEADS, RET_QK_DIM), positions)
    rk = rotary(rk.reshape(b, t, RET_HEADS, RET_QK_DIM), positions) * (RET_QK_DIM ** -0.5)
    rv = rv.reshape(b, t, RET_HEADS, 1, RET_V_DIM)
    log_gamma = jnp.log1p(-jnp.exp2(-5.0 - jnp.arange(RET_HEADS, dtype=jnp.float32)))
    la = jnp.broadcast_to(log_gamma[:, None], (b, t, RET_HEADS, 1))
    o = chunked_decay_scan(rq, rk, rv, la).reshape(b, t, RET_HEADS, RET_V_DIM)
    o = rms_norm(o).reshape(b, t, RET_V_WIDTH) * jax.nn.silu(rg)
    y_ret = o @ w_ret_out

    g_ssm, g_ret = jnp.split(jax.nn.sigmoid(gate_pre + b_gate), N_BRANCHES, axis=-1)
    return (g_ssm * y_ssm + g_ret * y_ret) @ w_o


def peer_ffn(h, w_query, sub_keys, expert_u, expert_v):
    b, t, d = h.shape
    q = (h @ w_query).reshape(b, t, PEER_HEADS, PEER_QUERY_DIM)
    s1 = jnp.einsum('bthd,hkd->bthk', q[..., :PEER_HALF], sub_keys[:, 0])
    s2 = jnp.einsum('bthd,hkd->bthk', q[..., PEER_HALF:], sub_keys[:, 1])
    v1, i1 = lax.top_k(s1, PEER_TOPK)
    v2, i2 = lax.top_k(s2, PEER_TOPK)
    n_cand = PEER_TOPK * PEER_TOPK
    cand_s = (v1[..., :, None] + v2[..., None, :]).reshape(b, t, PEER_HEADS, n_cand)
    cand_id = (i1[..., :, None] * PEER_N_KEYS + i2[..., None, :]).reshape(b, t, PEER_HEADS, n_cand)
    top_s, top_pos = lax.top_k(cand_s, PEER_TOPK)
    ids = jnp.take_along_axis(cand_id, top_pos, axis=-1)
    gates = jax.nn.softmax(top_s.astype(jnp.float32), axis=-1).astype(h.dtype)
    n_blk = (b * t) // PEER_TOKEN_BLOCK
    sel = PEER_HEADS * PEER_TOPK

    def expert_block(args):
        xb, idb, gb = args
        act = jax.nn.gelu(jnp.einsum('td,tsd->ts', xb, expert_u[idb]), approximate=False)
        return jnp.einsum('ts,tsd->td', gb * act, expert_v[idb])

    out = lax.map(expert_block, (h.reshape(n_blk, PEER_TOKEN_BLOCK, d), ids.reshape(n_blk, PEER_TOKEN_BLOCK, sel), gates.reshape(n_blk, PEER_TOKEN_BLOCK, sel)))
    return out.reshape(b, t, d)


def setup_inputs(seed: int = 0) -> dict:
    key = jax.random.key(seed)
    ks = jax.random.split(key, 24)
    f32 = jnp.float32

    def nrm(k, shape, scale):
        return jax.random.normal(k, shape, f32) * scale

    x = nrm(ks[0], (BATCH, SEQ, D_MODEL), 1.0)
    positions = jnp.broadcast_to(jnp.arange(SEQ, dtype=jnp.int32), (BATCH, SEQ))
    norm_mix = 1.0 + nrm(ks[1], (DEPTH, D_MODEL), 0.02)
    w_in = nrm(ks[2], (DEPTH, D_MODEL, PROJ_WIDTH), D_MODEL ** -0.5)
    conv_w = nrm(ks[3], (DEPTH, CONV_WIDTH, SSM_XBC), CONV_WIDTH ** -0.5)
    conv_b = nrm(ks[4], (DEPTH, SSM_XBC), 0.02)
    u = jax.random.uniform(ks[5], (DEPTH, SSM_HEADS), f32)
    dt0 = jnp.exp(u * (jnp.log(0.1) - jnp.log(0.001)) + jnp.log(0.001))
    dt_bias = dt0 + jnp.log(-jnp.expm1(-dt0))
    a_log = jnp.log(jax.random.uniform(ks[6], (DEPTH, SSM_HEADS), f32, 1.0, 16.0))
    d_skip = 1.0 + nrm(ks[7], (DEPTH, SSM_HEADS), 0.1)
    ssm_norm = 1.0 + nrm(ks[8], (DEPTH, SSM_D_INNER), 0.02)
    w_ssm_out = nrm(ks[9], (DEPTH, SSM_D_INNER, D_MODEL), SSM_D_INNER ** -0.5)
    w_ret_out = nrm(ks[10], (DEPTH, RET_V_WIDTH, D_MODEL), RET_V_WIDTH ** -0.5)
    b_gate = nrm(ks[11], (DEPTH, N_BRANCHES * D_MODEL), 0.02)
    w_o = nrm(ks[12], (DEPTH, D_MODEL, D_MODEL), D_MODEL ** -0.5)
    norm_ffn = 1.0 + nrm(ks[13], (DEPTH, D_MODEL), 0.02)
    w_query = nrm(ks[14], (DEPTH, D_MODEL, PEER_HEADS * PEER_QUERY_DIM), D_MODEL ** -0.5)
    sub_keys = nrm(ks[15], (DEPTH, PEER_HEADS, 2, PEER_N_KEYS, PEER_HALF), PEER_HALF ** -0.5)
    expert_u = nrm(ks[16], (DEPTH, PEER_EXPERTS, D_MODEL), D_MODEL ** -0.5)
    expert_v = nrm(ks[17], (DEPTH, PEER_EXPERTS, D_MODEL), PEER_HEADS ** -0.5)
    norm_final = 1.0 + nrm(ks[18], (D_MODEL,), 0.02)
    return {"x": x, "positions": positions, "norm_mix": norm_mix, "w_in": w_in, "conv_w": conv_w, "conv_b": conv_b, "dt_bias": dt_bias, "a_log": a_log, "d_skip": d_skip, "ssm_norm": ssm_norm, "w_ssm_out": w_ssm_out, "w_ret_out": w_ret_out, "b_gate": b_gate, "w_o": w_o, "norm_ffn": norm_ffn, "w_query": w_query, "sub_keys": sub_keys, "expert_u": expert_u, "expert_v": expert_v, "norm_final": norm_final}


def reference(x, positions, norm_mix, w_in, conv_w, conv_b, dt_bias, a_log, d_skip, ssm_norm, w_ssm_out, w_ret_out, b_gate, w_o, norm_ffn, w_query, sub_keys, expert_u, expert_v, norm_final):
    for l in range(DEPTH):
        h = rms_norm(x, norm_mix[l])
        x = x + hybrid_mixer(h, positions, w_in[l], conv_w[l], conv_b[l], dt_bias[l], a_log[l], d_skip[l], ssm_norm[l], w_ssm_out[l], w_ret_out[l], b_gate[l], w_o[l])
        h = rms_norm(x, norm_ffn[l])
        x = x + peer_ffn(h, w_query[l], sub_keys[l], expert_u[l], expert_v[l])
    return rms_norm(x, norm_final)
```

```python
import functools

import jax
import jax.numpy as jnp
import numpy as np
from jax import lax
from jax.experimental import pallas as pl
from jax.experimental.pallas import tpu as pltpu

F32 = jnp.float32
BF16 = jnp.bfloat16

D_MODEL = 2048
CHUNK = 128
SSM_D_INNER = 2 * D_MODEL
SSM_HEAD_DIM = 64
SSM_HEADS = SSM_D_INNER // SSM_HEAD_DIM
SSM_GROUPS = 8
SSM_HPG = SSM_HEADS // SSM_GROUPS
SSM_STATE = 128
SSM_GROUP_WIDTH = SSM_D_INNER // SSM_GROUPS
CONV_WIDTH = 4
RET_HEADS = 8
RET_QK_DIM = D_MODEL // RET_HEADS
RET_V_DIM = 2 * RET_QK_DIM
RET_QK_WIDTH = RET_HEADS * RET_QK_DIM
RET_V_WIDTH = RET_HEADS * RET_V_DIM
ROPE_BASE = 10000.0
PEER_HEADS = 8
PEER_N_KEYS = 128
PEER_EXPERTS = PEER_N_KEYS * PEER_N_KEYS
PEER_TOPK = 16
PEER_QUERY_DIM = 256
PEER_HALF = PEER_QUERY_DIM // 2
PEER_SEL = PEER_HEADS * PEER_TOPK
NORM_EPS = 1e-6

LANES = 128
SUBLANES = 8
VMEM_LIMIT_BYTES = 56 * 1024 * 1024

OFF_Z = 0
OFF_XS = OFF_Z + SSM_D_INNER
OFF_RV = OFF_XS + SSM_D_INNER
OFF_RG = OFF_RV + RET_V_WIDTH
OFF_GATE = OFF_RG + RET_V_WIDTH
OFF_RQ = OFF_GATE + 2 * D_MODEL
OFF_RK = OFF_RQ + RET_QK_WIDTH
OFF_B = OFF_RK + RET_QK_WIDTH
OFF_C = OFF_B + SSM_GROUPS * SSM_STATE
PROJ_MAIN = OFF_C + SSM_GROUPS * SSM_STATE
DT_PAD = LANES

HIGHEST = lax.Precision.HIGHEST


def _cparams(semantics):
    return pltpu.CompilerParams(dimension_semantics=semantics, vmem_limit_bytes=VMEM_LIMIT_BYTES)


def _silu(v):
    return v * (1.0 / (1.0 + jnp.exp(-v)))


def _sigmoid(v):
    return 1.0 / (1.0 + jnp.exp(-v))


def _softplus(v):
    return jnp.maximum(v, 0.0) + jnp.log(1.0 + jnp.exp(-jnp.abs(v)))


def _norm_matmul_kernel(x_ref, nw_ref, w_ref, o_ref, *rest, emit_h):
    if emit_h:
        h_out_ref, h_ref = rest
    else:
        (h_ref,) = rest

    @pl.when(pl.program_id(1) == 0)
    def _():
        x = x_ref[...]
        ms = jnp.mean(x * x, axis=-1, keepdims=True)
        h = x * lax.rsqrt(ms + NORM_EPS) * nw_ref[...]
        h_ref[...] = h.astype(BF16)
        if emit_h:
            h_out_ref[...] = h

    o_ref[...] = jnp.dot(h_ref[...], w_ref[...], preferred_element_type=F32)


def norm_matmul(x, nw, w, *, tm, tn, emit_h=False):
    t, d = x.shape
    n = w.shape[1]
    out_shape = [jax.ShapeDtypeStruct((t, n), F32)]
    out_specs = [pl.BlockSpec((tm, tn), lambda i, j: (i, j))]
    if emit_h:
        out_shape.append(jax.ShapeDtypeStruct((t, d), F32))
        out_specs.append(pl.BlockSpec((tm, d), lambda i, j: (i, 0)))
    res = pl.pallas_call(
        functools.partial(_norm_matmul_kernel, emit_h=emit_h),
        grid=(t // tm, n // tn),
        in_specs=[
            pl.BlockSpec((tm, d), lambda i, j: (i, 0)),
            pl.BlockSpec((1, d), lambda i, j: (0, 0)),
            pl.BlockSpec((d, tn), lambda i, j: (0, j)),
        ],
        out_specs=out_specs,
        out_shape=out_shape,
        scratch_shapes=[pltpu.VMEM((tm, d), BF16)],
        compiler_params=_cparams(("arbitrary", "arbitrary")),
        name="norm_matmul",
    )(x, nw.reshape(1, d), w)
    return res if emit_h else res[0]


def _causal_conv_silu(x, tail_ref, w_ref, b_ref):
    tail = tail_ref[...]
    w = w_ref[...]
    row8 = lax.broadcasted_iota(jnp.int32, (SUBLANES, x.shape[1]), 0)
    acc = x * w[CONV_WIDTH - 1:CONV_WIDTH, :] + b_ref[...]
    for s in range(1, CONV_WIDTH):
        rolled = pltpu.roll(x, s, axis=0)
        head = jnp.where(row8 < s, pltpu.roll(tail, s, axis=0), rolled[:SUBLANES])
        shifted = jnp.concatenate([head, rolled[SUBLANES:]], axis=0)
        acc = acc + shifted * w[CONV_WIDTH - 1 - s:CONV_WIDTH - s, :]
    tail_ref[...] = x[CHUNK - SUBLANES:]
    return _silu(acc)


def _ssd_kernel(z_ref, xs_ref, b_ref, c_ref, dt_ref,
                cw_xs_ref, cb_xs_ref, cw_b_ref, cb_b_ref, cw_c_ref, cb_c_ref,
                dtb_ref, alog_ref, dskip_ref, nw_ref, expand_ref,
                y_ref,
                tail_xs, tail_b, tail_c, state_ref):
    @pl.when(pl.program_id(0) == 0)
    def _():
        tail_xs[...] = jnp.zeros_like(tail_xs)
        tail_b[...] = jnp.zeros_like(tail_b)
        tail_c[...] = jnp.zeros_like(tail_c)
        state_ref[...] = jnp.zeros_like(state_ref)

    xs = _causal_conv_silu(xs_ref[...], tail_xs, cw_xs_ref, cb_xs_ref)
    bm = _causal_conv_silu(b_ref[...], tail_b, cw_b_ref, cb_b_ref)
    cm = _causal_conv_silu(c_ref[...], tail_c, cw_c_ref, cb_c_ref)

    dt = _softplus(dt_ref[...] + dtb_ref[...])
    la = dt * (-jnp.exp(alog_ref[...]))
    row = lax.broadcasted_iota(jnp.int32, (CHUNK, CHUNK), 0)
    col = lax.broadcasted_iota(jnp.int32, (CHUNK, CHUNK), 1)
    causal = row >= col
    cum = jnp.dot(causal.astype(F32), la, precision=HIGHEST, preferred_element_type=F32)
    cum_t = cum.T
    total = cum[CHUNK - 1:CHUNK, :]

    expand = expand_ref[...]
    dt_x = jnp.dot(dt, expand, precision=HIGHEST, preferred_element_type=F32)
    ecum_x = jnp.dot(jnp.exp(cum), expand, precision=HIGHEST, preferred_element_type=F32)
    edec_x = jnp.dot(jnp.exp(total - cum), expand, precision=HIGHEST, preferred_element_type=F32)
    xdt = xs * dt_x
    v_dec = (xdt * edec_x).astype(BF16)
    xdt_b = xdt.astype(BF16)
    etot_x = ecum_x[CHUNK - 1:CHUNK, :]

    lane = lax.broadcasted_iota(jnp.int32, (CHUNK, 2 * SSM_HEAD_DIM), 1)
    left = lane < SSM_HEAD_DIM
    y_parts = []
    for g in range(SSM_GROUPS):
        ns = slice(g * SSM_STATE, (g + 1) * SSM_STATE)
        gs = slice(g * SSM_GROUP_WIDTH, (g + 1) * SSM_GROUP_WIDTH)
        c_g = cm[:, ns].astype(BF16)
        b_g = bm[:, ns].astype(BF16)
        scores = lax.dot_general(c_g, b_g, (((1,), (1,)), ((), ())), preferred_element_type=F32)
        prev = state_ref[g]
        y_inter = jnp.dot(c_g, prev.astype(BF16), preferred_element_type=F32) * ecum_x[:, gs]
        intra = []
        for pr in range(SSM_HPG // 2):
            cs = slice(g * SSM_GROUP_WIDTH + pr * 2 * SSM_HEAD_DIM, g * SSM_GROUP_WIDTH + (pr + 1) * 2 * SSM_HEAD_DIM)
            v_pair = xdt_b[:, cs]
            outs = []
            for sub in range(2):
                hd = g * SSM_HPG + pr * 2 + sub
                seg = cum[:, hd:hd + 1] - cum_t[hd:hd + 1, :]
                decay = jnp.exp(jnp.where(causal, seg, -1e30))
                m = (scores * decay).astype(BF16)
                outs.append(jnp.dot(m, v_pair, preferred_element_type=F32))
            intra.append(jnp.where(left, outs[0], outs[1]))
        y_g = jnp.concatenate(intra, axis=1) + y_inter
        st = lax.dot_general(b_g, v_dec[:, gs], (((0,), (0,)), ((), ())), preferred_element_type=F32)
        state_ref[g] = etot_x[:, gs] * prev + st
        y_g = y_g + xs[:, gs] * dskip_ref[:, gs]
        y_g = y_g * _silu(z_ref[:, gs])
        ms = jnp.mean(y_g * y_g, axis=-1, keepdims=True)
        y_parts.append(y_g * lax.rsqrt(ms + NORM_EPS) * nw_ref[:, gs])
    y_ref[...] = jnp.concatenate(y_parts, axis=1).astype(y_ref.dtype)


def ssd_branch(proj, dt_raw, cw, cb, dt_bias, a_log, d_skip, ssm_norm):
    t = proj.shape[0]
    nst = SSM_GROUPS * SSM_STATE
    cw_xs, cw_b, cw_c = cw[:, :SSM_D_INNER], cw[:, SSM_D_INNER:SSM_D_INNER + nst], cw[:, SSM_D_INNER + nst:]
    cb = cb.reshape(1, -1)
    cb_xs, cb_b, cb_c = cb[:, :SSM_D_INNER], cb[:, SSM_D_INNER:SSM_D_INNER + nst], cb[:, SSM_D_INNER + nst:]
    pad = DT_PAD - SSM_HEADS
    dtb = jnp.pad(dt_bias, (0, pad)).reshape(1, DT_PAD)
    alog = jnp.pad(a_log, (0, pad)).reshape(1, DT_PAD)
    dskip = jnp.repeat(d_skip, SSM_HEAD_DIM).reshape(1, SSM_D_INNER)
    expand = (np.arange(DT_PAD)[:, None] == (np.arange(SSM_D_INNER)[None, :] // SSM_HEAD_DIM)).astype(np.float32)

    def whole(a):
        return pl.BlockSpec(a.shape, lambda c: (0,) * a.ndim)

    consts = [cw_xs, cb_xs, cw_b, cb_b, cw_c, cb_c, dtb, alog, dskip, ssm_norm.reshape(1, -1), jnp.asarray(expand)]
    return pl.pallas_call(
        _ssd_kernel,
        grid=(t // CHUNK,),
        in_specs=[
            pl.BlockSpec((CHUNK, SSM_D_INNER), lambda c: (c, OFF_Z // SSM_D_INNER)),
            pl.BlockSpec((CHUNK, SSM_D_INNER), lambda c: (c, OFF_XS // SSM_D_INNER)),
            pl.BlockSpec((CHUNK, nst), lambda c: (c, OFF_B // nst)),
            pl.BlockSpec((CHUNK, nst), lambda c: (c, OFF_C // nst)),
            pl.BlockSpec((CHUNK, DT_PAD), lambda c: (c, 0)),
        ] + [whole(a) for a in consts],
        out_specs=pl.BlockSpec((CHUNK, SSM_D_INNER), lambda c: (c, 0)),
        out_shape=jax.ShapeDtypeStruct((t, SSM_D_INNER), BF16),
        scratch_shapes=[
            pltpu.VMEM((SUBLANES, SSM_D_INNER), F32),
            pltpu.VMEM((SUBLANES, nst), F32),
            pltpu.VMEM((SUBLANES, nst), F32),
            pltpu.VMEM((SSM_GROUPS, SSM_STATE, SSM_GROUP_WIDTH), F32),
        ],
        compiler_params=_cparams(("arbitrary",)),
        name="ssd_branch",
    )(proj, proj, proj, proj, dt_raw, *consts)


def _ret_log_gamma(h):
    return float(np.log1p(-np.exp2(-5.0 - h)))


def _rotary(x, cos, sin):
    half = RET_QK_DIM // 2
    x1, x2 = x[:, :half], x[:, half:]
    return jnp.concatenate([x1 * cos - x2 * sin, x2 * cos + x1 * sin], axis=1)


def _ret_kernel(q_ref, k_ref, v_ref, g_ref, pos_ref, freq_ref, o_ref, state_ref):
    @pl.when(pl.program_id(0) == 0)
    def _():
        state_ref[...] = jnp.zeros_like(state_ref)

    ang = pos_ref[...].astype(F32) * freq_ref[...]
    cos, sin = jnp.cos(ang), jnp.sin(ang)
    row = lax.broadcasted_iota(jnp.int32, (CHUNK, CHUNK), 0)
    col = lax.broadcasted_iota(jnp.int32, (CHUNK, CHUNK), 1)
    causal = row >= col
    rel = (row - col).astype(F32)
    r1 = lax.broadcasted_iota(jnp.int32, (CHUNK, 1), 0).astype(F32)
    outs = []
    for h in range(RET_HEADS):
        lg = _ret_log_gamma(h)
        qs = slice(h * RET_QK_DIM, (h + 1) * RET_QK_DIM)
        vs = slice(h * RET_V_DIM, (h + 1) * RET_V_DIM)
        q = _rotary(q_ref[:, qs], cos, sin).astype(BF16)
        k = (_rotary(k_ref[:, qs], cos, sin) * (RET_QK_DIM ** -0.5)).astype(BF16)
        v = v_ref[:, vs]
        scores = lax.dot_general(q, k, (((1,), (1,)), ((), ())), preferred_element_type=F32)
        decay = jnp.exp(jnp.where(causal, rel * lg, -1e30))
        o = jnp.dot((scores * decay).astype(BF16), v.astype(BF16), preferred_element_type=F32)
        prev = state_ref[h]
        o = o + jnp.dot(q, prev.astype(BF16), preferred_element_type=F32) * jnp.exp((r1 + 1.0) * lg)
        v_dec = (v * jnp.exp((CHUNK - 1.0 - r1) * lg)).astype(BF16)
        st = lax.dot_general(k, v_dec, (((0,), (0,)), ((), ())), preferred_element_type=F32)
        state_ref[h] = float(np.exp(CHUNK * lg)) * prev + st
        ms = jnp.mean(o * o, axis=-1, keepdims=True)
        outs.append(o * lax.rsqrt(ms + NORM_EPS) * _silu(g_ref[:, vs]))
    o_ref[...] = jnp.concatenate(outs, axis=1).astype(o_ref.dtype)


def retention_branch(proj, positions):
    t = proj.shape[0]
    half = RET_QK_DIM // 2
    inv_freq = (1.0 / (ROPE_BASE ** (np.arange(half, dtype=np.float32) / half))).astype(np.float32)
    return pl.pallas_call(
        _ret_kernel,
        grid=(t // CHUNK,),
        in_specs=[
            pl.BlockSpec((CHUNK, RET_QK_WIDTH), lambda c: (c, OFF_RQ // RET_QK_WIDTH)),
            pl.BlockSpec((CHUNK, RET_QK_WIDTH), lambda c: (c, OFF_RK // RET_QK_WIDTH)),
            pl.BlockSpec((CHUNK, RET_V_WIDTH), lambda c: (c, OFF_RV // RET_V_WIDTH)),
            pl.BlockSpec((CHUNK, RET_V_WIDTH), lambda c: (c, OFF_RG // RET_V_WIDTH)),
            pl.BlockSpec((CHUNK, 1), lambda c: (c, 0)),
            pl.BlockSpec((1, half), lambda c: (0, 0)),
        ],
        out_specs=pl.BlockSpec((CHUNK, RET_V_WIDTH), lambda c: (c, 0)),
        out_shape=jax.ShapeDtypeStruct((t, RET_V_WIDTH), BF16),
        scratch_shapes=[pltpu.VMEM((RET_HEADS, RET_QK_DIM, RET_V_DIM), F32)],
        compiler_params=_cparams(("arbitrary",)),
        name="retention_branch",
    )(proj, proj, proj, proj, positions.reshape(t, 1), jnp.asarray(inv_freq).reshape(1, half))


def _merge_kernel(y_ref, o_ref, ws_ref, wr_ref, gs_ref, gr_ref, bs_ref, br_ref, m_ref):
    y_ssm = jnp.dot(y_ref[...], ws_ref[...], preferred_element_type=F32)
    y_ret = jnp.dot(o_ref[...], wr_ref[...], preferred_element_type=F32)
    m = _sigmoid(gs_ref[...] + bs_ref[...]) * y_ssm + _sigmoid(gr_ref[...] + br_ref[...]) * y_ret
    m_ref[...] = m.astype(m_ref.dtype)


def gated_merge(y, o, w_ssm_out, w_ret_out, proj, b_gate, *, tm, tn):
    t = y.shape[0]
    nj = D_MODEL // tn
    bg = b_gate.reshape(1, 2 * D_MODEL)
    return pl.pallas_call(
        _merge_kernel,
        grid=(t // tm, nj),
        in_specs=[
            pl.BlockSpec((tm, SSM_D_INNER), lambda i, j: (i, 0)),
            pl.BlockSpec((tm, RET_V_WIDTH), lambda i, j: (i, 0)),
            pl.BlockSpec((SSM_D_INNER, tn), lambda i, j: (0, j)),
            pl.BlockSpec((RET_V_WIDTH, tn), lambda i, j: (0, j)),
            pl.BlockSpec((tm, tn), lambda i, j: (i, OFF_GATE // tn + j)),
            pl.BlockSpec((tm, tn), lambda i, j: (i, OFF_GATE // tn + nj + j)),
            pl.BlockSpec((1, tn), lambda i, j: (0, j)),
            pl.BlockSpec((1, tn), lambda i, j: (0, nj + j)),
        ],
        out_specs=pl.BlockSpec((tm, tn), lambda i, j: (i, j)),
        out_shape=jax.ShapeDtypeStruct((t, D_MODEL), BF16),
        compiler_params=_cparams(("arbitrary", "arbitrary")),
        name="gated_merge",
    )(y, o, w_ssm_out, w_ret_out, proj, proj, bg, bg)


def _out_proj_kernel(m_ref, w_ref, x_ref, o_ref):
    o_ref[...] = x_ref[...] + jnp.dot(m_ref[...], w_ref[...], preferred_element_type=F32)


def out_proj_residual(m, w_o, x, *, tm, tn):
    t = m.shape[0]
    return pl.pallas_call(
        _out_proj_kernel,
        grid=(t // tm, D_MODEL // tn),
        in_specs=[
            pl.BlockSpec((tm, D_MODEL), lambda i, j: (i, 0)),
            pl.BlockSpec((D_MODEL, tn), lambda i, j: (0, j)),
            pl.BlockSpec((tm, tn), lambda i, j: (i, j)),
        ],
        out_specs=pl.BlockSpec((tm, tn), lambda i, j: (i, j)),
        out_shape=jax.ShapeDtypeStruct((t, D_MODEL), F32),
        compiler_params=_cparams(("arbitrary", "arbitrary")),
        name="out_proj_residual",
    )(m, w_o, x)


def _top_k_rows(vals, k):
    n, w = vals.shape
    row = lax.broadcasted_iota(jnp.int32, (n, w), 0)
    out_row = lax.broadcasted_iota(jnp.int32, (k, w), 0)
    top_v = jnp.zeros((k, w), F32)
    top_i = jnp.zeros((k, w), jnp.int32)
    for i in range(k):
        m = jnp.max(vals, axis=0, keepdims=True)
        sel = jnp.min(jnp.where(vals == m, row, n), axis=0, keepdims=True)
        top_v = jnp.where(out_row == i, m, top_v)
        top_i = jnp.where(out_row == i, sel, top_i)
        vals = jnp.where(row == sel, -jnp.inf, vals)
    return top_v, top_i


def _take_rows(table, idx):
    out = jnp.zeros(idx.shape, table.dtype)
    for a in range(table.shape[0]):
        out = jnp.where(idx == a, table[a:a + 1, :], out)
    return out


def _route_kernel(q_ref, keys_ref, ids_ref, gates_ref):
    q = q_ref[...]
    ids, gates = [], []
    for h in range(PEER_HEADS):
        q1 = q[:, h * PEER_QUERY_DIM:h * PEER_QUERY_DIM + PEER_HALF]
        q2 = q[:, h * PEER_QUERY_DIM + PEER_HALF:(h + 1) * PEER_QUERY_DIM]
        s1 = lax.dot_general(keys_ref[h, 0], q1, (((1,), (1,)), ((), ())), preferred_element_type=F32)
        s2 = lax.dot_general(keys_ref[h, 1], q2, (((1,), (1,)), ((), ())), preferred_element_type=F32)
        v1, i1 = _top_k_rows(s1, PEER_TOPK)
        v2, i2 = _top_k_rows(s2, PEER_TOPK)
        cand = jnp.concatenate([v1[a:a + 1, :] + v2 for a in range(PEER_TOPK)], axis=0)
        top_s, top_pos = _top_k_rows(cand, PEER_TOPK)
        e1 = _take_rows(i1, top_pos >> 4)
        e2 = _take_rows(i2, top_pos & (PEER_TOPK - 1))
        ids.append(e1 * PEER_N_KEYS + e2)
        ex = jnp.exp(top_s - top_s[0:1, :])
        gates.append(ex / jnp.sum(ex, axis=0, keepdims=True))
    ids_ref[...] = jnp.concatenate(ids, axis=0).T
    gates_ref[...] = jnp.concatenate(gates, axis=0).T


def peer_route(q, sub_keys, *, tt):
    t = q.shape[0]
    return pl.pallas_call(
        _route_kernel,
        grid=(t // tt,),
        in_specs=[
            pl.BlockSpec((tt, q.shape[1]), lambda i: (i, 0)),
            pl.BlockSpec(sub_keys.shape, lambda i: (0, 0, 0, 0)),
        ],
        out_specs=[
            pl.BlockSpec((tt, PEER_SEL), lambda i: (i, 0)),
            pl.BlockSpec((tt, PEER_SEL), lambda i: (i, 0)),
        ],
        out_shape=[
            jax.ShapeDtypeStruct((t, PEER_SEL), jnp.int32),
            jax.ShapeDtypeStruct((t, PEER_SEL), F32),
        ],
        compiler_params=_cparams(("arbitrary",)),
        name="peer_route",
    )(q, sub_keys)


ROW_TILES = D_MODEL // LANES
PEER_TB = 8
PEER_ROWS = PEER_TB * PEER_SEL
PEER_KC = 32
ISSUE_UNROLL = 8


def _erf(x):
    x = jnp.clip(x, -4.0, 4.0)
    x2 = x * x
    p = -2.72614225801306e-10
    for c in (2.77068142495902e-08, -2.10102402082508e-06, -5.69250639462346e-05,
              -7.34990630326855e-04, -2.95459980854025e-03, -1.60960333262415e-02):
        p = p * x2 + c
    q = -1.45660718464996e-05
    for c in (-2.13374055278905e-04, -1.68282697438203e-03, -7.37332916720468e-03, -1.42647390514189e-02):
        q = q * x2 + c
    return x * p / q


def _gelu_exact(x):
    return 0.5 * x * (1.0 + _erf(x * float(1.0 / np.sqrt(2.0))))


def _peer_issue(table_ref, ids_ref, row0, buf, sem):
    def body(it, carry):
        for u in range(ISSUE_UNROLL):
            p = it * ISSUE_UNROLL + u
            e = ids_ref[row0 + p // PEER_SEL, p % PEER_SEL]
            pltpu.make_async_copy(table_ref.at[e], buf.at[p], sem).start()
        return carry
    lax.fori_loop(0, PEER_ROWS // ISSUE_UNROLL, body, 0)


def _peer_wait(table_ref, buf, sem):
    pltpu.make_async_copy(table_ref.at[pl.ds(0, PEER_ROWS)], buf, sem).wait()


def _peer_compute(buf, h_ref, g_ref, x_ref, o_ref, row0, ones_ref):
    ones = ones_ref[...]
    lane = lax.broadcasted_iota(jnp.int32, (PEER_KC, SUBLANES, LANES), 2)
    kidx = lax.broadcasted_iota(jnp.int32, (PEER_KC, SUBLANES, LANES), 0)
    s_rows = []
    for t in range(PEER_TB):
        h = h_ref[row0 + t]
        z = jnp.zeros((SUBLANES, LANES), F32)
        for c in range(PEER_SEL // PEER_KC):
            base = t * PEER_SEL + c * PEER_KC
            u = buf[base:base + PEER_KC, 0:ROW_TILES, :]
            p = u * h[None]
            p8 = p[:, :SUBLANES] + p[:, SUBLANES:]
            r = jnp.dot(p8.reshape(PEER_KC * SUBLANES, LANES), ones, preferred_element_type=F32)
            r = r.reshape(PEER_KC, SUBLANES, LANES)
            z = z + jnp.sum(jnp.where(lane == kidx + c * PEER_KC, r, 0.0), axis=0)
        s_rows.append(jnp.sum(z, axis=0, keepdims=True))
    s = jnp.concatenate(s_rows, axis=0)
    w = g_ref[pl.ds(row0, PEER_TB), :] * _gelu_exact(s)
    for t in range(PEER_TB):
        w_t = w[t:t + 1, :]
        acc = jnp.zeros((2, SUBLANES, LANES), F32)
        for c in range(PEER_SEL // PEER_KC):
            base = t * PEER_SEL + c * PEER_KC
            a = jnp.where(lane == kidx + c * PEER_KC, w_t[None], 0.0)
            wsp = jnp.dot(a.reshape(PEER_KC * SUBLANES, LANES), ones, precision=HIGHEST,
                          preferred_element_type=F32).reshape(PEER_KC, 1, SUBLANES, LANES)
            v = buf[base:base + PEER_KC, ROW_TILES:2 * ROW_TILES, :].reshape(PEER_KC, 2, SUBLANES, LANES)
            acc = acc + jnp.sum(v * wsp, axis=0)
        o_ref[row0 + t] = x_ref[row0 + t] + acc.reshape(ROW_TILES, LANES)


def _peer_kernel(ids_cur, ids_nxt, table_ref, h_ref, g_ref, x_ref, ones_ref, o_ref, buf0, buf1, sem0, sem1):
    i = pl.program_id(0)
    n = pl.num_programs(0)

    @pl.when(i == 0)
    def _():
        _peer_issue(table_ref, ids_cur, 0, buf0, sem0)

    _peer_issue(table_ref, ids_cur, PEER_TB, buf1, sem1)
    _peer_wait(table_ref, buf0, sem0)
    _peer_compute(buf0, h_ref, g_ref, x_ref, o_ref, 0, ones_ref)

    @pl.when(i + 1 < n)
    def _():
        _peer_issue(table_ref, ids_nxt, 0, buf0, sem0)

    _peer_wait(table_ref, buf1, sem1)
    _peer_compute(buf1, h_ref, g_ref, x_ref, o_ref, PEER_TB, ones_ref)


def peer_experts(ids, gates, table, h, x):
    t = h.shape[0]
    tb2 = 2 * PEER_TB
    nblk = t // tb2
    h3 = h.reshape(t, ROW_TILES, LANES)
    x3 = x.reshape(t, ROW_TILES, LANES)
    ones = jnp.ones((LANES, LANES), F32)
    out = pl.pallas_call(
        _peer_kernel,
        grid=(nblk,),
        in_specs=[
            pl.BlockSpec((tb2, PEER_SEL), lambda i: (i, 0), memory_space=pltpu.SMEM),
            pl.BlockSpec((tb2, PEER_SEL), lambda i: (jnp.minimum(i + 1, nblk - 1), 0), memory_space=pltpu.SMEM),
            pl.BlockSpec(memory_space=pl.ANY),
            pl.BlockSpec((tb2, ROW_TILES, LANES), lambda i: (i, 0, 0)),
            pl.BlockSpec((tb2, PEER_SEL), lambda i: (i, 0)),
            pl.BlockSpec((tb2, ROW_TILES, LANES), lambda i: (i, 0, 0)),
            pl.BlockSpec((LANES, LANES), lambda i: (0, 0)),
        ],
        out_specs=pl.BlockSpec((tb2, ROW_TILES, LANES), lambda i: (i, 0, 0)),
        out_shape=jax.ShapeDtypeStruct((t, ROW_TILES, LANES), F32),
        scratch_shapes=[
            pltpu.VMEM((PEER_ROWS, 2 * ROW_TILES, LANES), F32),
            pltpu.VMEM((PEER_ROWS, 2 * ROW_TILES, LANES), F32),
            pltpu.SemaphoreType.DMA,
            pltpu.SemaphoreType.DMA,
        ],
        compiler_params=_cparams(("arbitrary",)),
        name="peer_experts",
    )(ids, ids, table, h3, gates, x3, ones)
    return out.reshape(t, D_MODEL)


def _final_norm_kernel(x_ref, w_ref, o_ref):
    x = x_ref[...]
    ms = jnp.mean(x * x, axis=-1, keepdims=True)
    o_ref[...] = x * lax.rsqrt(ms + NORM_EPS) * w_ref[...]


def final_norm(x, w, *, tm):
    t, d = x.shape
    return pl.pallas_call(
        _final_norm_kernel,
        grid=(t // tm,),
        in_specs=[pl.BlockSpec((tm, d), lambda i: (i, 0)), pl.BlockSpec((1, d), lambda i: (0, 0))],
        out_specs=pl.BlockSpec((tm, d), lambda i: (i, 0)),
        out_shape=jax.ShapeDtypeStruct((t, d), F32),
        compiler_params=_cparams(("arbitrary",)),
        name="final_norm",
    )(x, w.reshape(1, d))


def _split_w_in(w_in):
    sizes = (SSM_D_INNER, SSM_D_INNER + 2 * SSM_GROUPS * SSM_STATE, SSM_HEADS, RET_QK_WIDTH, RET_QK_WIDTH,
             RET_V_WIDTH, RET_V_WIDTH, 2 * D_MODEL)
    pts = np.cumsum(sizes)[:-1].tolist()
    z, xbc, dt, rq, rk, rv, rg, gate = jnp.split(w_in, pts, axis=1)
    nst = SSM_GROUPS * SSM_STATE
    xs, bmat, cmat = xbc[:, :SSM_D_INNER], xbc[:, SSM_D_INNER:SSM_D_INNER + nst], xbc[:, SSM_D_INNER + nst:]
    main = jnp.concatenate([z, xs, rv, rg, gate, rq, rk, bmat, cmat], axis=1).astype(BF16)
    dt = jnp.pad(dt, ((0, 0), (0, DT_PAD - SSM_HEADS))).astype(BF16)
    return main, dt


def _mixer_layer(x, positions, norm_mix, w_in, conv_w, conv_b, dt_bias, a_log, d_skip, ssm_norm,
                 w_ssm_out, w_ret_out, b_gate, w_o):
    w_main, w_dt = _split_w_in(w_in)
    proj = norm_matmul(x, norm_mix, w_main, tm=1024, tn=1024)
    dt_raw = norm_matmul(x, norm_mix, w_dt, tm=1024, tn=DT_PAD)
    y = ssd_branch(proj, dt_raw, conv_w, conv_b, dt_bias, a_log, d_skip, ssm_norm)
    o = retention_branch(proj, positions)
    m = gated_merge(y, o, w_ssm_out.astype(BF16), w_ret_out.astype(BF16), proj, b_gate, tm=512, tn=512)
    return out_proj_residual(m, w_o.astype(BF16), x, tm=1024, tn=1024)


def _peer_layer(x, norm_ffn, w_query, sub_keys, expert_u, expert_v):
    q, h = norm_matmul(x, norm_ffn, w_query.astype(BF16), tm=1024, tn=1024, emit_h=True)
    ids, gates = peer_route(q, sub_keys, tt=256)
    table = jnp.concatenate([expert_u.reshape(PEER_EXPERTS, ROW_TILES, LANES),
                             expert_v.reshape(PEER_EXPERTS, ROW_TILES, LANES)], axis=1)
    return peer_experts(ids, gates, table, h, x)


def kernel(x, positions, norm_mix, w_in, conv_w, conv_b, dt_bias, a_log, d_skip, ssm_norm, w_ssm_out, w_ret_out,
           b_gate, w_o, norm_ffn, w_query, sub_keys, expert_u, expert_v, norm_final):
    b, t, d = x.shape
    depth = norm_mix.shape[0]
    outs = []
    for bi in range(b):
        xb = x[bi]
        pos = positions[bi]
        for l in range(depth):
            xb = _mixer_layer(xb, pos, norm_mix[l], w_in[l], conv_w[l], conv_b[l], dt_bias[l], a_log[l], d_skip[l],
                              ssm_norm[l], w_ssm_out[l], w_ret_out[l], b_gate[l], w_o[l])
            xb = _peer_layer(xb, norm_ffn[l], w_query[l], sub_keys[l], expert_u[l], expert_v[l])
        outs.append(final_norm(xb, norm_final, tm=512))
    return jnp.stack(outs, axis=0)
```

```python
import functools

import jax
import jax.numpy as jnp
import numpy as np
from jax import lax
from jax.experimental import pallas as pl
from jax.experimental.pallas import tpu as pltpu

F32 = jnp.float32
BF16 = jnp.bfloat16

D_MODEL = 2048
CHUNK = 128
SSM_D_INNER = 2 * D_MODEL
SSM_HEAD_DIM = 64
SSM_HEADS = SSM_D_INNER // SSM_HEAD_DIM
SSM_GROUPS = 8
SSM_HPG = SSM_HEADS // SSM_GROUPS
SSM_STATE = 128
SSM_GROUP_WIDTH = SSM_D_INNER // SSM_GROUPS
CONV_WIDTH = 4
RET_HEADS = 8
RET_QK_DIM = D_MODEL // RET_HEADS
RET_V_DIM = 2 * RET_QK_DIM
RET_QK_WIDTH = RET_HEADS * RET_QK_DIM
RET_V_WIDTH = RET_HEADS * RET_V_DIM
ROPE_BASE = 10000.0
PEER_HEADS = 8
PEER_N_KEYS = 128
PEER_EXPERTS = PEER_N_KEYS * PEER_N_KEYS
PEER_TOPK = 16
PEER_QUERY_DIM = 256
PEER_HALF = PEER_QUERY_DIM // 2
PEER_SEL = PEER_HEADS * PEER_TOPK
NORM_EPS = 1e-6

LANES = 128
SUBLANES = 8
VMEM_LIMIT_BYTES = 56 * 1024 * 1024

OFF_Z = 0
OFF_XS = OFF_Z + SSM_D_INNER
OFF_RV = OFF_XS + SSM_D_INNER
OFF_RG = OFF_RV + RET_V_WIDTH
OFF_GATE = OFF_RG + RET_V_WIDTH
OFF_RQ = OFF_GATE + 2 * D_MODEL
OFF_RK = OFF_RQ + RET_QK_WIDTH
OFF_B = OFF_RK + RET_QK_WIDTH
OFF_C = OFF_B + SSM_GROUPS * SSM_STATE
PROJ_MAIN = OFF_C + SSM_GROUPS * SSM_STATE
DT_PAD = LANES

HIGHEST = lax.Precision.HIGHEST


def _cparams(semantics):
    return pltpu.CompilerParams(dimension_semantics=semantics, vmem_limit_bytes=VMEM_LIMIT_BYTES)


def _silu(v):
    return v * (1.0 / (1.0 + jnp.exp(-v)))


def _sigmoid(v):
    return 1.0 / (1.0 + jnp.exp(-v))


def _softplus(v):
    return jnp.maximum(v, 0.0) + jnp.log(1.0 + jnp.exp(-jnp.abs(v)))


def _norm_matmul_kernel(x_ref, nw_ref, w_ref, o_ref, *rest, emit_h):
    if emit_h:
        h_out_ref, h_ref = rest
    else:
        (h_ref,) = rest

    @pl.when(pl.program_id(1) == 0)
    def _():
        x = x_ref[...]
        ms = jnp.mean(x * x, axis=-1, keepdims=True)
        h = x * lax.rsqrt(ms + NORM_EPS) * nw_ref[...]
        h_ref[...] = h.astype(BF16)
        if emit_h:
            h_out_ref[...] = h

    o_ref[...] = jnp.dot(h_ref[...], w_ref[...], preferred_element_type=F32)


def norm_matmul(x, nw, w, *, tm, tn, emit_h=False):
    t, d = x.shape
    n = w.shape[1]
    out_shape = [jax.ShapeDtypeStruct((t, n), F32)]
    out_specs = [pl.BlockSpec((tm, tn), lambda i, j: (i, j))]
    if emit_h:
        out_shape.append(jax.ShapeDtypeStruct((t, d), F32))
        out_specs.append(pl.BlockSpec((tm, d), lambda i, j: (i, 0)))
    res = pl.pallas_call(
        functools.partial(_norm_matmul_kernel, emit_h=emit_h),
        grid=(t // tm, n // tn),
        in_specs=[
            pl.BlockSpec((tm, d), lambda i, j: (i, 0)),
            pl.BlockSpec((1, d), lambda i, j: (0, 0)),
            pl.BlockSpec((d, tn), lambda i, j: (0, j)),
        ],
        out_specs=out_specs,
        out_shape=out_shape,
        scratch_shapes=[pltpu.VMEM((tm, d), BF16)],
        compiler_params=_cparams(("arbitrary", "arbitrary")),
        name="norm_matmul",
    )(x, nw.reshape(1, d), w)
    return res if emit_h else res[0]


def _causal_conv_silu(x, tail_ref, w_ref, b_ref):
    tail = tail_ref[...]
    w = w_ref[...]
    row8 = lax.broadcasted_iota(jnp.int32, (SUBLANES, x.shape[1]), 0)
    acc = x * w[CONV_WIDTH - 1:CONV_WIDTH, :] + b_ref[...]
    for s in range(1, CONV_WIDTH):
        rolled = pltpu.roll(x, s, axis=0)
        head = jnp.where(row8 < s, pltpu.roll(tail, s, axis=0), rolled[:SUBLANES])
        shifted = jnp.concatenate([head, rolled[SUBLANES:]], axis=0)
        acc = acc + shifted * w[CONV_WIDTH - 1 - s:CONV_WIDTH - s, :]
    tail_ref[...] = x[CHUNK - SUBLANES:]
    return _silu(acc)


def _ssd_kernel(z_ref, xs_ref, b_ref, c_ref, dt_ref,
                cw_xs_ref, cb_xs_ref, cw_b_ref, cb_b_ref, cw_c_ref, cb_c_ref,
                dtb_ref, alog_ref, dskip_ref, nw_ref, expand_ref,
                y_ref,
                tail_xs, tail_b, tail_c, state_ref):
    @pl.when(pl.program_id(0) == 0)
    def _():
        tail_xs[...] = jnp.zeros_like(tail_xs)
        tail_b[...] = jnp.zeros_like(tail_b)
        tail_c[...] = jnp.zeros_like(tail_c)
        state_ref[...] = jnp.zeros_like(state_ref)

    xs = _causal_conv_silu(xs_ref[...], tail_xs, cw_xs_ref, cb_xs_ref)
    bm = _causal_conv_silu(b_ref[...], tail_b, cw_b_ref, cb_b_ref)
    cm = _causal_conv_silu(c_ref[...], tail_c, cw_c_ref, cb_c_ref)

    dt = _softplus(dt_ref[...] + dtb_ref[...])
    la = dt * (-jnp.exp(alog_ref[...]))
    row = lax.broadcasted_iota(jnp.int32, (CHUNK, CHUNK), 0)
    col = lax.broadcasted_iota(jnp.int32, (CHUNK, CHUNK), 1)
    causal = row >= col
    cum = jnp.dot(causal.astype(F32), la, precision=HIGHEST, preferred_element_type=F32)
    cum_t = cum.T
    total = cum[CHUNK - 1:CHUNK, :]

    expand = expand_ref[...]
    dt_x = jnp.dot(dt, expand, precision=HIGHEST, preferred_element_type=F32)
    ecum_x = jnp.dot(jnp.exp(cum), expand, precision=HIGHEST, preferred_element_type=F32)
    edec_x = jnp.dot(jnp.exp(total - cum), expand, precision=HIGHEST, preferred_element_type=F32)
    xdt = xs * dt_x
    v_dec = (xdt * edec_x).astype(BF16)
    xdt_b = xdt.astype(BF16)
    etot_x = ecum_x[CHUNK - 1:CHUNK, :]

    lane = lax.broadcasted_iota(jnp.int32, (CHUNK, 2 * SSM_HEAD_DIM), 1)
    left = lane < SSM_HEAD_DIM
    y_parts = []
    for g in range(SSM_GROUPS):
        ns = slice(g * SSM_STATE, (g + 1) * SSM_STATE)
        gs = slice(g * SSM_GROUP_WIDTH, (g + 1) * SSM_GROUP_WIDTH)
        c_g = cm[:, ns].astype(BF16)
        b_g = bm[:, ns].astype(BF16)
        scores = lax.dot_general(c_g, b_g, (((1,), (1,)), ((), ())), preferred_element_type=F32)
        prev = state_ref[g]
        y_inter = jnp.dot(c_g, prev.astype(BF16), preferred_element_type=F32) * ecum_x[:, gs]
        intra = []
        for pr in range(SSM_HPG // 2):
            cs = slice(g * SSM_GROUP_WIDTH + pr * 2 * SSM_HEAD_DIM, g * SSM_GROUP_WIDTH + (pr + 1) * 2 * SSM_HEAD_DIM)
            v_pair = xdt_b[:, cs]
            outs = []
            for sub in range(2):
                hd = g * SSM_HPG + pr * 2 + sub
                seg = cum[:, hd:hd + 1] - cum_t[hd:hd + 1, :]
                decay = jnp.exp(jnp.where(causal, seg, -1e30))
                m = (scores * decay).astype(BF16)
                outs.append(jnp.dot(m, v_pair, preferred_element_type=F32))
            intra.append(jnp.where(left, outs[0], outs[1]))
        y_g = jnp.concatenate(intra, axis=1) + y_inter
        st = lax.dot_general(b_g, v_dec[:, gs], (((0,), (0,)), ((), ())), preferred_element_type=F32)
        state_ref[g] = etot_x[:, gs] * prev + st
        y_g = y_g + xs[:, gs] * dskip_ref[:, gs]
        y_g = y_g * _silu(z_ref[:, gs])
        ms = jnp.mean(y_g * y_g, axis=-1, keepdims=True)
        y_parts.append(y_g * lax.rsqrt(ms + NORM_EPS) * nw_ref[:, gs])
    y_ref[...] = jnp.concatenate(y_parts, axis=1).astype(y_ref.dtype)


def ssd_branch(proj, dt_raw, cw, cb, dt_bias, a_log, d_skip, ssm_norm):
    t = proj.shape[0]
    nst = SSM_GROUPS * SSM_STATE
    cw_xs, cw_b, cw_c = cw[:, :SSM_D_INNER], cw[:, SSM_D_INNER:SSM_D_INNER + nst], cw[:, SSM_D_INNER + nst:]
    cb = cb.reshape(1, -1)
    cb_xs, cb_b, cb_c = cb[:, :SSM_D_INNER], cb[:, SSM_D_INNER:SSM_D_INNER + nst], cb[:, SSM_D_INNER + nst:]
    pad = DT_PAD - SSM_HEADS
    dtb = jnp.pad(dt_bias, (0, pad)).reshape(1, DT_PAD)
    alog = jnp.pad(a_log, (0, pad)).reshape(1, DT_PAD)
    dskip = jnp.repeat(d_skip, SSM_HEAD_DIM).reshape(1, SSM_D_INNER)
    expand = (np.arange(DT_PAD)[:, None] == (np.arange(SSM_D_INNER)[None, :] // SSM_HEAD_DIM)).astype(np.float32)

    def whole(a):
        return pl.BlockSpec(a.shape, lambda c: (0,) * a.ndim)

    consts = [cw_xs, cb_xs, cw_b, cb_b, cw_c, cb_c, dtb, alog, dskip, ssm_norm.reshape(1, -1), jnp.asarray(expand)]
    return pl.pallas_call(
        _ssd_kernel,
        grid=(t // CHUNK,),
        in_specs=[
            pl.BlockSpec((CHUNK, SSM_D_INNER), lambda c: (c, OFF_Z // SSM_D_INNER)),
            pl.BlockSpec((CHUNK, SSM_D_INNER), lambda c: (c, OFF_XS // SSM_D_INNER)),
            pl.BlockSpec((CHUNK, nst), lambda c: (c, OFF_B // nst)),
            pl.BlockSpec((CHUNK, nst), lambda c: (c, OFF_C // nst)),
            pl.BlockSpec((CHUNK, DT_PAD), lambda c: (c, 0)),
        ] + [whole(a) for a in consts],
        out_specs=pl.BlockSpec((CHUNK, SSM_D_INNER), lambda c: (c, 0)),
        out_shape=jax.ShapeDtypeStruct((t, SSM_D_INNER), BF16),
        scratch_shapes=[
            pltpu.VMEM((SUBLANES, SSM_D_INNER), F32),
            pltpu.VMEM((SUBLANES, nst), F32),
            pltpu.VMEM((SUBLANES, nst), F32),
            pltpu.VMEM((SSM_GROUPS, SSM_STATE, SSM_GROUP_WIDTH), F32),
        ],
        compiler_params=_cparams(("arbitrary",)),
        name="ssd_branch",
    )(proj, proj, proj, proj, dt_raw, *consts)


def _ret_log_gamma(h):
    return float(np.log1p(-np.exp2(-5.0 - h)))


def _rotary(x, cos, sin):
    half = RET_QK_DIM // 2
    x1, x2 = x[:, :half], x[:, half:]
    return jnp.concatenate([x1 * cos - x2 * sin, x2 * cos + x1 * sin], axis=1)


def _ret_kernel(q_ref, k_ref, v_ref, g_ref, pos_ref, freq_ref, o_ref, state_ref):
    @pl.when(pl.program_id(0) == 0)
    def _():
        state_ref[...] = jnp.zeros_like(state_ref)

    ang = pos_ref[...].astype(F32) * freq_ref[...]
    cos, sin = jnp.cos(ang), jnp.sin(ang)
    row = lax.broadcasted_iota(jnp.int32, (CHUNK, CHUNK), 0)
    col = lax.broadcasted_iota(jnp.int32, (CHUNK, CHUNK), 1)
    causal = row >= col
    rel = (row - col).astype(F32)
    r1 = lax.broadcasted_iota(jnp.int32, (CHUNK, 1), 0).astype(F32)
    outs = []
    for h in range(RET_HEADS):
        lg = _ret_log_gamma(h)
        qs = slice(h * RET_QK_DIM, (h + 1) * RET_QK_DIM)
        vs = slice(h * RET_V_DIM, (h + 1) * RET_V_DIM)
        q = _rotary(q_ref[:, qs], cos, sin).astype(BF16)
        k = (_rotary(k_ref[:, qs], cos, sin) * (RET_QK_DIM ** -0.5)).astype(BF16)
        v = v_ref[:, vs]
        scores = lax.dot_general(q, k, (((1,), (1,)), ((), ())), preferred_element_type=F32)
        decay = jnp.exp(jnp.where(causal, rel * lg, -1e30))
        o = jnp.dot((scores * decay).astype(BF16), v.astype(BF16), preferred_element_type=F32)
        prev = state_ref[h]
        o = o + jnp.dot(q, prev.astype(BF16), preferred_element_type=F32) * jnp.exp((r1 + 1.0) * lg)
        v_dec = (v * jnp.exp((CHUNK - 1.0 - r1) * lg)).astype(BF16)
        st = lax.dot_general(k, v_dec, (((0,), (0,)), ((), ())), preferred_element_type=F32)
        state_ref[h] = float(np.exp(CHUNK * lg)) * prev + st
        ms = jnp.mean(o * o, axis=-1, keepdims=True)
        outs.append(o * lax.rsqrt(ms + NORM_EPS) * _silu(g_ref[:, vs]))
    o_ref[...] = jnp.concatenate(outs, axis=1).astype(o_ref.dtype)


def retention_branch(proj, positions):
    t = proj.shape[0]
    half = RET_QK_DIM // 2
    inv_freq = (1.0 / (ROPE_BASE ** (np.arange(half, dtype=np.float32) / half))).astype(np.float32)
    return pl.pallas_call(
        _ret_kernel,
        grid=(t // CHUNK,),
        in_specs=[
            pl.BlockSpec((CHUNK, RET_QK_WIDTH), lambda c: (c, OFF_RQ // RET_QK_WIDTH)),
            pl.BlockSpec((CHUNK, RET_QK_WIDTH), lambda c: (c, OFF_RK // RET_QK_WIDTH)),
            pl.BlockSpec((CHUNK, RET_V_WIDTH), lambda c: (c, OFF_RV // RET_V_WIDTH)),
            pl.BlockSpec((CHUNK, RET_V_WIDTH), lambda c: (c, OFF_RG // RET_V_WIDTH)),
            pl.BlockSpec((CHUNK, 1), lambda c: (c, 0)),
            pl.BlockSpec((1, half), lambda c: (0, 0)),
        ],
        out_specs=pl.BlockSpec((CHUNK, RET_V_WIDTH), lambda c: (c, 0)),
        out_shape=jax.ShapeDtypeStruct((t, RET_V_WIDTH), BF16),
        scratch_shapes=[pltpu.VMEM((RET_HEADS, RET_QK_DIM, RET_V_DIM), F32)],
        compiler_params=_cparams(("arbitrary",)),
        name="retention_branch",
    )(proj, proj, proj, proj, positions.reshape(t, 1), jnp.asarray(inv_freq).reshape(1, half))


def _merge_kernel(y_ref, o_ref, ws_ref, wr_ref, gs_ref, gr_ref, bs_ref, br_ref, m_ref):
    y_ssm = jnp.dot(y_ref[...], ws_ref[...], preferred_element_type=F32)
    y_ret = jnp.dot(o_ref[...], wr_ref[...], preferred_element_type=F32)
    m = _sigmoid(gs_ref[...] + bs_ref[...]) * y_ssm + _sigmoid(gr_ref[...] + br_ref[...]) * y_ret
    m_ref[...] = m.astype(m_ref.dtype)


def gated_merge(y, o, w_ssm_out, w_ret_out, proj, b_gate, *, tm, tn):
    t = y.shape[0]
    nj = D_MODEL // tn
    bg = b_gate.reshape(1, 2 * D_MODEL)
    return pl.pallas_call(
        _merge_kernel,
        grid=(t // tm, nj),
        in_specs=[
            pl.BlockSpec((tm, SSM_D_INNER), lambda i, j: (i, 0)),
            pl.BlockSpec((tm, RET_V_WIDTH), lambda i, j: (i, 0)),
            pl.BlockSpec((SSM_D_INNER, tn), lambda i, j: (0, j)),
            pl.BlockSpec((RET_V_WIDTH, tn), lambda i, j: (0, j)),
            pl.BlockSpec((tm, tn), lambda i, j: (i, OFF_GATE // tn + j)),
            pl.BlockSpec((tm, tn), lambda i, j: (i, OFF_GATE // tn + nj + j)),
            pl.BlockSpec((1, tn), lambda i, j: (0, j)),
            pl.BlockSpec((1, tn), lambda i, j: (0, nj + j)),
        ],
        out_specs=pl.BlockSpec((tm, tn), lambda i, j: (i, j)),
        out_shape=jax.ShapeDtypeStruct((t, D_MODEL), BF16),
        compiler_params=_cparams(("arbitrary", "arbitrary")),
        name="gated_merge",
    )(y, o, w_ssm_out, w_ret_out, proj, proj, bg, bg)


def _out_proj_kernel(m_ref, w_ref, x_ref, o_ref):
    o_ref[...] = x_ref[...] + jnp.dot(m_ref[...], w_ref[...], preferred_element_type=F32)


def out_proj_residual(m, w_o, x, *, tm, tn):
    t = m.shape[0]
    return pl.pallas_call(
        _out_proj_kernel,
        grid=(t // tm, D_MODEL // tn),
        in_specs=[
            pl.BlockSpec((tm, D_MODEL), lambda i, j: (i, 0)),
            pl.BlockSpec((D_MODEL, tn), lambda i, j: (0, j)),
            pl.BlockSpec((tm, tn), lambda i, j: (i, j)),
        ],
        out_specs=pl.BlockSpec((tm, tn), lambda i, j: (i, j)),
        out_shape=jax.ShapeDtypeStruct((t, D_MODEL), F32),
        compiler_params=_cparams(("arbitrary", "arbitrary")),
        name="out_proj_residual",
    )(m, w_o, x)


def _top_k_rows(vals, k):
    n, w = vals.shape
    row = lax.broadcasted_iota(jnp.int32, (n, w), 0)
    out_row = lax.broadcasted_iota(jnp.int32, (k, w), 0)
    top_v = jnp.zeros((k, w), F32)
    top_i = jnp.zeros((k, w), jnp.int32)
    for i in range(k):
        m = jnp.max(vals, axis=0, keepdims=True)
        sel = jnp.min(jnp.where(vals == m, row, n), axis=0, keepdims=True)
        top_v = jnp.where(out_row == i, m, top_v)
        top_i = jnp.where(out_row == i, sel, top_i)
        vals = jnp.where(row == sel, -jnp.inf, vals)
    return top_v, top_i


def _take_rows(table, idx):
    out = jnp.zeros(idx.shape, table.dtype)
    for a in range(table.shape[0]):
        out = jnp.where(idx == a, table[a:a + 1, :], out)
    return out


def _route_kernel(q_ref, keys_ref, ids_ref, gates_ref):
    q = q_ref[...]
    ids, gates = [], []
    for h in range(PEER_HEADS):
        q1 = q[:, h * PEER_QUERY_DIM:h * PEER_QUERY_DIM + PEER_HALF]
        q2 = q[:, h * PEER_QUERY_DIM + PEER_HALF:(h + 1) * PEER_QUERY_DIM]
        s1 = lax.dot_general(keys_ref[h, 0], q1, (((1,), (1,)), ((), ())), preferred_element_type=F32)
        s2 = lax.dot_general(keys_ref[h, 1], q2, (((1,), (1,)), ((), ())), preferred_element_type=F32)
        v1, i1 = _top_k_rows(s1, PEER_TOPK)
        v2, i2 = _top_k_rows(s2, PEER_TOPK)
        cand = jnp.concatenate([v1[a:a + 1, :] + v2 for a in range(PEER_TOPK)], axis=0)
        top_s, top_pos = _top_k_rows(cand, PEER_TOPK)
        e1 = _take_rows(i1, top_pos >> 4)
        e2 = _take_rows(i2, top_pos & (PEER_TOPK - 1))
        ids.append(e1 * PEER_N_KEYS + e2)
        ex = jnp.exp(top_s - top_s[0:1, :])
        gates.append(ex / jnp.sum(ex, axis=0, keepdims=True))
    ids_ref[...] = jnp.concatenate(ids, axis=0).T
    gates_ref[...] = jnp.concatenate(gates, axis=0).T


def peer_route(q, sub_keys, *, tt):
    t = q.shape[0]
    return pl.pallas_call(
        _route_kernel,
        grid=(t // tt,),
        in_specs=[
            pl.BlockSpec((tt, q.shape[1]), lambda i: (i, 0)),
            pl.BlockSpec(sub_keys.shape, lambda i: (0, 0, 0, 0)),
        ],
        out_specs=[
            pl.BlockSpec((tt, PEER_SEL), lambda i: (i, 0)),
            pl.BlockSpec((tt, PEER_SEL), lambda i: (i, 0)),
        ],
        out_shape=[
            jax.ShapeDtypeStruct((t, PEER_SEL), jnp.int32),
            jax.ShapeDtypeStruct((t, PEER_SEL), F32),
        ],
        compiler_params=_cparams(("arbitrary",)),
        name="peer_route",
    )(q, sub_keys)


ROW_TILES = D_MODEL // LANES
PEER_TB = 8
PEER_ROWS = PEER_TB * PEER_SEL
PEER_KC = 16
PEER_NBUF = 4
PEER_AHEAD = 2


def _erf(x):
    x = jnp.clip(x, -4.0, 4.0)
    x2 = x * x
    p = -2.72614225801306e-10
    for c in (2.77068142495902e-08, -2.10102402082508e-06, -5.69250639462346e-05,
              -7.34990630326855e-04, -2.95459980854025e-03, -1.60960333262415e-02):
        p = p * x2 + c
    q = -1.45660718464996e-05
    for c in (-2.13374055278905e-04, -1.68282697438203e-03, -7.37332916720468e-03, -1.42647390514189e-02):
        q = q * x2 + c
    return x * p / q


def _gelu_exact(x):
    return 0.5 * x * (1.0 + _erf(x * float(1.0 / np.sqrt(2.0))))


class _Prefetch:
    def __init__(self, table_ref, ids_ref, ids_row0, buf, sem):
        self.table_ref, self.ids_ref, self.ids_row0, self.buf, self.sem = table_ref, ids_ref, ids_row0, buf, sem

    def issue(self, t, k0, k1):
        for k in range(k0, k1):
            e = self.ids_ref[self.ids_row0 + t, k]
            pltpu.make_async_copy(self.table_ref.at[e], self.buf.at[t * PEER_SEL + k], self.sem).start(priority=k % 2)

    def issue_all(self):
        def body(t, carry):
            for k in range(PEER_SEL):
                e = self.ids_ref[self.ids_row0 + t, k]
                pltpu.make_async_copy(self.table_ref.at[e], self.buf.at[t * PEER_SEL + k], self.sem).start(priority=k % 2)
            return carry
        lax.fori_loop(0, PEER_TB, body, 0)

    def wait(self):
        pltpu.make_async_copy(self.table_ref.at[pl.ds(0, PEER_ROWS)], self.buf, self.sem).wait()


def _peer_compute(buf, h_ref, g_ref, x_ref, o_ref, row0, ones_ref, nxt):
    ones = ones_ref[...]
    lane = lax.broadcasted_iota(jnp.int32, (PEER_KC, SUBLANES, LANES), 2)
    kidx = lax.broadcasted_iota(jnp.int32, (PEER_KC, SUBLANES, LANES), 0)
    half = PEER_SEL // 2
    s_rows = []
    for t in range(PEER_TB):
        h = h_ref[row0 + t]
        z = jnp.zeros((SUBLANES, LANES), F32)
        for c in range(PEER_SEL // PEER_KC):
            base = t * PEER_SEL + c * PEER_KC
            u = buf[base:base + PEER_KC, 0:ROW_TILES, :].astype(F32)
            p = u * h[None]
            p8 = p[:, :SUBLANES] + p[:, SUBLANES:]
            r = jnp.dot(p8.reshape(PEER_KC * SUBLANES, LANES), ones, preferred_element_type=F32)
            r = r.reshape(PEER_KC, SUBLANES, LANES)
            z = z + jnp.sum(jnp.where(lane == kidx + c * PEER_KC, r, 0.0), axis=0)
        s_rows.append(jnp.sum(z, axis=0, keepdims=True))
        nxt.issue(t, 0, half)
    s = jnp.concatenate(s_rows, axis=0)
    w = g_ref[pl.ds(row0, PEER_TB), :] * _gelu_exact(s)
    for t in range(PEER_TB):
        w_t = w[t:t + 1, :]
        acc = jnp.zeros((2, SUBLANES, LANES), F32)
        for c in range(PEER_SEL // PEER_KC):
            base = t * PEER_SEL + c * PEER_KC
            a = jnp.where(lane == kidx + c * PEER_KC, w_t[None], 0.0)
            wsp = jnp.dot(a.reshape(PEER_KC * SUBLANES, LANES), ones,
                          preferred_element_type=F32).reshape(PEER_KC, 1, SUBLANES, LANES)
            v = buf[base:base + PEER_KC, ROW_TILES:2 * ROW_TILES, :].astype(F32)
            acc = acc + jnp.sum(v.reshape(PEER_KC, 2, SUBLANES, LANES) * wsp, axis=0)
        o_ref[row0 + t] = x_ref[row0 + t] + acc.reshape(ROW_TILES, LANES)
        nxt.issue(t, half, PEER_SEL)


def _peer_kernel(ids_cur, ids_nxt, table_ref, h_ref, g_ref, x_ref, ones_ref, o_ref, *scratch):
    bufs, sems = scratch[:PEER_NBUF], scratch[PEER_NBUF:]
    i = pl.program_id(0)
    cur = [_Prefetch(table_ref, ids_cur, b * PEER_TB, bufs[b], sems[b]) for b in range(PEER_NBUF)]
    nxt = [_Prefetch(table_ref, ids_nxt, b * PEER_TB, bufs[b], sems[b]) for b in range(PEER_AHEAD)]

    @pl.when(i == 0)
    def _():
        for b in range(PEER_AHEAD):
            cur[b].issue_all()

    for b in range(PEER_NBUF):
        cur[b].wait()
        ahead = cur[b + PEER_AHEAD] if b + PEER_AHEAD < PEER_NBUF else nxt[b + PEER_AHEAD - PEER_NBUF]
        _peer_compute(bufs[b], h_ref, g_ref, x_ref, o_ref, b * PEER_TB, ones_ref, ahead)

    @pl.when(i == pl.num_programs(0) - 1)
    def _():
        for b in range(PEER_AHEAD):
            nxt[b].wait()


def peer_experts(ids, gates, table, h, x):
    t = h.shape[0]
    tb2 = PEER_NBUF * PEER_TB
    nblk = t // tb2
    h3 = h.reshape(t, ROW_TILES, LANES)
    x3 = x.reshape(t, ROW_TILES, LANES)
    ones = jnp.ones((LANES, LANES), F32)
    out = pl.pallas_call(
        _peer_kernel,
        grid=(nblk,),
        in_specs=[
            pl.BlockSpec((tb2, PEER_SEL), lambda i: (i, 0), memory_space=pltpu.SMEM),
            pl.BlockSpec((tb2, PEER_SEL), lambda i: (jnp.minimum(i + 1, nblk - 1), 0), memory_space=pltpu.SMEM),
            pl.BlockSpec(memory_space=pl.ANY),
            pl.BlockSpec((tb2, ROW_TILES, LANES), lambda i: (i, 0, 0)),
            pl.BlockSpec((tb2, PEER_SEL), lambda i: (i, 0)),
            pl.BlockSpec((tb2, ROW_TILES, LANES), lambda i: (i, 0, 0)),
            pl.BlockSpec((LANES, LANES), lambda i: (0, 0)),
        ],
        out_specs=pl.BlockSpec((tb2, ROW_TILES, LANES), lambda i: (i, 0, 0)),
        out_shape=jax.ShapeDtypeStruct((t, ROW_TILES, LANES), F32),
        scratch_shapes=([pltpu.VMEM((PEER_ROWS, 2 * ROW_TILES, LANES), table.dtype)] * PEER_NBUF
                        + [pltpu.SemaphoreType.DMA] * PEER_NBUF),
        compiler_params=_cparams(("arbitrary",)),
        name="peer_experts",
    )(ids, ids, table, h3, gates, x3, ones)
    return out.reshape(t, D_MODEL)


def _final_norm_kernel(x_ref, w_ref, o_ref):
    x = x_ref[...]
    ms = jnp.mean(x * x, axis=-1, keepdims=True)
    o_ref[...] = x * lax.rsqrt(ms + NORM_EPS) * w_ref[...]


def final_norm(x, w, *, tm):
    t, d = x.shape
    return pl.pallas_call(
        _final_norm_kernel,
        grid=(t // tm,),
        in_specs=[pl.BlockSpec((tm, d), lambda i: (i, 0)), pl.BlockSpec((1, d), lambda i: (0, 0))],
        out_specs=pl.BlockSpec((tm, d), lambda i: (i, 0)),
        out_shape=jax.ShapeDtypeStruct((t, d), F32),
        compiler_params=_cparams(("arbitrary",)),
        name="final_norm",
    )(x, w.reshape(1, d))


def _split_w_in(w_in):
    sizes = (SSM_D_INNER, SSM_D_INNER + 2 * SSM_GROUPS * SSM_STATE, SSM_HEADS, RET_QK_WIDTH, RET_QK_WIDTH,
             RET_V_WIDTH, RET_V_WIDTH, 2 * D_MODEL)
    pts = np.cumsum(sizes)[:-1].tolist()
    z, xbc, dt, rq, rk, rv, rg, gate = jnp.split(w_in, pts, axis=1)
    nst = SSM_GROUPS * SSM_STATE
    xs, bmat, cmat = xbc[:, :SSM_D_INNER], xbc[:, SSM_D_INNER:SSM_D_INNER + nst], xbc[:, SSM_D_INNER + nst:]
    main = jnp.concatenate([z, xs, rv, rg, gate, rq, rk, bmat, cmat], axis=1).astype(BF16)
    dt = jnp.pad(dt, ((0, 0), (0, DT_PAD - SSM_HEADS))).astype(BF16)
    return main, dt


def _mixer_layer(x, positions, norm_mix, w_in, conv_w, conv_b, dt_bias, a_log, d_skip, ssm_norm,
                 w_ssm_out, w_ret_out, b_gate, w_o):
    w_main, w_dt = _split_w_in(w_in)
    proj = norm_matmul(x, norm_mix, w_main, tm=1024, tn=1024)
    dt_raw = norm_matmul(x, norm_mix, w_dt, tm=1024, tn=DT_PAD)
    y = ssd_branch(proj, dt_raw, conv_w, conv_b, dt_bias, a_log, d_skip, ssm_norm)
    o = retention_branch(proj, positions)
    m = gated_merge(y, o, w_ssm_out.astype(BF16), w_ret_out.astype(BF16), proj, b_gate, tm=512, tn=512)
    return out_proj_residual(m, w_o.astype(BF16), x, tm=1024, tn=1024)


def _peer_layer(x, norm_ffn, w_query, sub_keys, expert_u, expert_v):
    q, h = norm_matmul(x, norm_ffn, w_query.astype(BF16), tm=1024, tn=1024, emit_h=True)
    ids, gates = peer_route(q, sub_keys, tt=256)
    table = jnp.concatenate([expert_u.astype(BF16).reshape(PEER_EXPERTS, ROW_TILES, LANES),
                             expert_v.astype(BF16).reshape(PEER_EXPERTS, ROW_TILES, LANES)], axis=1)
    return peer_experts(ids, gates, table, h, x)


def kernel(x, positions, norm_mix, w_in, conv_w, conv_b, dt_bias, a_log, d_skip, ssm_norm, w_ssm_out, w_ret_out,
           b_gate, w_o, norm_ffn, w_query, sub_keys, expert_u, expert_v, norm_final):
    b, t, d = x.shape
    depth = norm_mix.shape[0]
    outs = []
    for bi in range(b):
        xb = x[bi]
        pos = positions[bi]
        for l in range(depth):
            xb = _mixer_layer(xb, pos, norm_mix[l], w_in[l], conv_w[l], conv_b[l], dt_bias[l], a_log[l], d_skip[l],
                              ssm_norm[l], w_ssm_out[l], w_ret_out[l], b_gate[l], w_o[l])
            xb = _peer_layer(xb, norm_ffn[l], w_query[l], sub_keys[l], expert_u[l], expert_v[l])
        outs.append(final_norm(xb, norm_final, tm=512))
    return jnp.stack(outs, axis=0)
```

```python
import functools

import jax
import jax.numpy as jnp
import numpy as np
from jax import lax
from jax.experimental import pallas as pl
from jax.experimental.pallas import tpu as pltpu

F32 = jnp.float32
BF16 = jnp.bfloat16

D_MODEL = 2048
CHUNK = 128
SSM_D_INNER = 2 * D_MODEL
SSM_HEAD_DIM = 64
SSM_HEADS = SSM_D_INNER // SSM_HEAD_DIM
SSM_GROUPS = 8
SSM_HPG = SSM_HEADS // SSM_GROUPS
SSM_STATE = 128
SSM_GROUP_WIDTH = SSM_D_INNER // SSM_GROUPS
CONV_WIDTH = 4
RET_HEADS = 8
RET_QK_DIM = D_MODEL // RET_HEADS
RET_V_DIM = 2 * RET_QK_DIM
RET_QK_WIDTH = RET_HEADS * RET_QK_DIM
RET_V_WIDTH = RET_HEADS * RET_V_DIM
ROPE_BASE = 10000.0
PEER_HEADS = 8
PEER_N_KEYS = 128
PEER_EXPERTS = PEER_N_KEYS * PEER_N_KEYS
PEER_TOPK = 16
PEER_QUERY_DIM = 256
PEER_HALF = PEER_QUERY_DIM // 2
PEER_SEL = PEER_HEADS * PEER_TOPK
NORM_EPS = 1e-6

LANES = 128
SUBLANES = 8
VMEM_LIMIT_BYTES = 56 * 1024 * 1024

SSM_NST = SSM_GROUPS * SSM_STATE
OFF_Z = 0
OFF_XS = OFF_Z + SSM_D_INNER
OFF_B = OFF_XS + SSM_D_INNER
OFF_C = OFF_B + SSM_NST
SSM_PART = OFF_C + SSM_NST
DT_PAD = LANES
OFF_RQ = 0
OFF_RK = OFF_RQ + RET_QK_WIDTH
OFF_RV = OFF_RK + RET_QK_WIDTH
OFF_RG = OFF_RV + RET_V_WIDTH
OFF_GATE = OFF_RG + RET_V_WIDTH
RET_PART = OFF_GATE + 2 * D_MODEL

def _cparams(semantics):
    return pltpu.CompilerParams(dimension_semantics=semantics, vmem_limit_bytes=VMEM_LIMIT_BYTES)


def _silu(v):
    return v * (1.0 / (1.0 + jnp.exp(-v)))


def _sigmoid(v):
    return 1.0 / (1.0 + jnp.exp(-v))


def _softplus(v):
    return jnp.maximum(v, 0.0) + jnp.log(1.0 + jnp.exp(-jnp.abs(v)))


def _norm_matmul_kernel(x_ref, nw_ref, w_ref, o_ref, *rest, emit_h):
    if emit_h:
        h_out_ref, h_ref = rest
    else:
        (h_ref,) = rest

    @pl.when(pl.program_id(1) == 0)
    def _():
        x = x_ref[...]
        ms = jnp.mean(x * x, axis=-1, keepdims=True)
        h = x * lax.rsqrt(ms + NORM_EPS) * nw_ref[...]
        h_ref[...] = h.astype(BF16)
        if emit_h:
            h_out_ref[...] = h

    o_ref[...] = jnp.dot(h_ref[...], w_ref[...], preferred_element_type=F32)


def norm_matmul(x, nw, w, *, tm, tn, emit_h=False):
    t, d = x.shape
    n = w.shape[1]
    out_shape = [jax.ShapeDtypeStruct((t, n), F32)]
    out_specs = [pl.BlockSpec((tm, tn), lambda i, j: (i, j))]
    if emit_h:
        out_shape.append(jax.ShapeDtypeStruct((t, d), F32))
        out_specs.append(pl.BlockSpec((tm, d), lambda i, j: (i, 0)))
    res = pl.pallas_call(
        functools.partial(_norm_matmul_kernel, emit_h=emit_h),
        grid=(t // tm, n // tn),
        in_specs=[
            pl.BlockSpec((tm, d), lambda i, j: (i, 0)),
            pl.BlockSpec((1, d), lambda i, j: (0, 0)),
            pl.BlockSpec((d, tn), lambda i, j: (0, j)),
        ],
        out_specs=out_specs,
        out_shape=out_shape,
        scratch_shapes=[pltpu.VMEM((tm, d), BF16)],
        compiler_params=_cparams(("arbitrary", "arbitrary")),
        name="norm_matmul",
    )(x, nw.reshape(1, d), w)
    return res if emit_h else res[0]


def _causal_conv_silu(x, tail_ref, w_ref, b_ref):
    tail = tail_ref[...]
    w = w_ref[...]
    row8 = lax.broadcasted_iota(jnp.int32, (SUBLANES, x.shape[1]), 0)
    acc = x * w[CONV_WIDTH - 1:CONV_WIDTH, :] + b_ref[...]
    for s in range(1, CONV_WIDTH):
        rolled = pltpu.roll(x, s, axis=0)
        head = jnp.where(row8 < s, pltpu.roll(tail, s, axis=0), rolled[:SUBLANES])
        shifted = jnp.concatenate([head, rolled[SUBLANES:]], axis=0)
        acc = acc + shifted * w[CONV_WIDTH - 1 - s:CONV_WIDTH - s, :]
    tail_ref[...] = x[CHUNK - SUBLANES:]
    return _silu(acc)


def _split3(a):
    hi = a.astype(BF16)
    r1 = a - hi.astype(F32)
    mid = r1.astype(BF16)
    lo = (r1 - mid.astype(F32)).astype(BF16)
    return hi, mid, lo


def _dot_right_01(a, m01):
    return sum(jnp.dot(part, m01, preferred_element_type=F32) for part in _split3(a))


def _dot_left_01(m01, a):
    return sum(jnp.dot(m01, part, preferred_element_type=F32) for part in _split3(a))


def _ssd_kernel(z_ref, xs_ref, b_ref, c_ref, dt_ref,
                cw_xs_ref, cb_xs_ref, cw_b_ref, cb_b_ref, cw_c_ref, cb_c_ref,
                dtb_ref, alog_ref, dskip_ref, nw_ref, expand_ref,
                y_ref,
                tail_xs, tail_b, tail_c, state_ref):
    @pl.when(pl.program_id(0) == 0)
    def _():
        tail_xs[...] = jnp.zeros_like(tail_xs)
        tail_b[...] = jnp.zeros_like(tail_b)
        tail_c[...] = jnp.zeros_like(tail_c)
        state_ref[...] = jnp.zeros_like(state_ref)

    xs = _causal_conv_silu(xs_ref[...], tail_xs, cw_xs_ref, cb_xs_ref)
    bm = _causal_conv_silu(b_ref[...], tail_b, cw_b_ref, cb_b_ref)
    cm = _causal_conv_silu(c_ref[...], tail_c, cw_c_ref, cb_c_ref)

    dt = _softplus(dt_ref[...] + dtb_ref[...])
    la = dt * (-jnp.exp(alog_ref[...]))
    row = lax.broadcasted_iota(jnp.int32, (CHUNK, CHUNK), 0)
    col = lax.broadcasted_iota(jnp.int32, (CHUNK, CHUNK), 1)
    causal = row >= col
    cum = _dot_left_01(causal.astype(BF16), la)
    cum_t = cum.T
    total = cum[CHUNK - 1:CHUNK, :]

    expand = expand_ref[...]
    dt_x = _dot_right_01(dt, expand)
    ecum_x = _dot_right_01(jnp.exp(cum), expand)
    edec_x = _dot_right_01(jnp.exp(total - cum), expand)
    xdt = xs * dt_x
    v_dec = (xdt * edec_x).astype(BF16)
    xdt_b = xdt.astype(BF16)
    etot_x = ecum_x[CHUNK - 1:CHUNK, :]

    lane = lax.broadcasted_iota(jnp.int32, (CHUNK, 2 * SSM_HEAD_DIM), 1)
    left = lane < SSM_HEAD_DIM
    y_parts = []
    for g in range(SSM_GROUPS):
        ns = slice(g * SSM_STATE, (g + 1) * SSM_STATE)
        gs = slice(g * SSM_GROUP_WIDTH, (g + 1) * SSM_GROUP_WIDTH)
        c_g = cm[:, ns].astype(BF16)
        b_g = bm[:, ns].astype(BF16)
        scores = lax.dot_general(c_g, b_g, (((1,), (1,)), ((), ())), preferred_element_type=F32)
        prev = state_ref[g]
        y_inter = jnp.dot(c_g, prev.astype(BF16), preferred_element_type=F32) * ecum_x[:, gs]
        intra = []
        for pr in range(SSM_HPG // 2):
            cs = slice(g * SSM_GROUP_WIDTH + pr * 2 * SSM_HEAD_DIM, g * SSM_GROUP_WIDTH + (pr + 1) * 2 * SSM_HEAD_DIM)
            v_pair = xdt_b[:, cs]
            outs = []
            for sub in range(2):
                hd = g * SSM_HPG + pr * 2 + sub
                seg = cum[:, hd:hd + 1] - cum_t[hd:hd + 1, :]
                decay = jnp.exp(jnp.where(causal, seg, -1e30))
                m = (scores * decay).astype(BF16)
                outs.append(jnp.dot(m, v_pair, preferred_element_type=F32))
            intra.append(jnp.where(left, outs[0], outs[1]))
        y_g = jnp.concatenate(intra, axis=1) + y_inter
        st = lax.dot_general(b_g, v_dec[:, gs], (((0,), (0,)), ((), ())), preferred_element_type=F32)
        state_ref[g] = etot_x[:, gs] * prev + st
        y_g = y_g + xs[:, gs] * dskip_ref[:, gs]
        y_g = y_g * _silu(z_ref[:, gs])
        ms = jnp.mean(y_g * y_g, axis=-1, keepdims=True)
        y_parts.append(y_g * lax.rsqrt(ms + NORM_EPS) * nw_ref[:, gs])
    y_ref[...] = jnp.concatenate(y_parts, axis=1).astype(y_ref.dtype)


def ssd_branch(proj, dt_raw, cw, cb, dt_bias, a_log, d_skip, ssm_norm):
    t = proj.shape[0]
    nst = SSM_GROUPS * SSM_STATE
    cw_xs, cw_b, cw_c = cw[:, :SSM_D_INNER], cw[:, SSM_D_INNER:SSM_D_INNER + nst], cw[:, SSM_D_INNER + nst:]
    cb = cb.reshape(1, -1)
    cb_xs, cb_b, cb_c = cb[:, :SSM_D_INNER], cb[:, SSM_D_INNER:SSM_D_INNER + nst], cb[:, SSM_D_INNER + nst:]
    pad = DT_PAD - SSM_HEADS
    dtb = jnp.pad(dt_bias, (0, pad)).reshape(1, DT_PAD)
    alog = jnp.pad(a_log, (0, pad)).reshape(1, DT_PAD)
    dskip = jnp.repeat(d_skip, SSM_HEAD_DIM).reshape(1, SSM_D_INNER)
    expand = (np.arange(DT_PAD)[:, None] == (np.arange(SSM_D_INNER)[None, :] // SSM_HEAD_DIM)).astype(np.float32)

    def whole(a):
        return pl.BlockSpec(a.shape, lambda c: (0,) * a.ndim)

    consts = [cw_xs, cb_xs, cw_b, cb_b, cw_c, cb_c, dtb, alog, dskip, ssm_norm.reshape(1, -1),
              jnp.asarray(expand, dtype=BF16)]
    return pl.pallas_call(
        _ssd_kernel,
        grid=(t // CHUNK,),
        in_specs=[
            pl.BlockSpec((CHUNK, SSM_D_INNER), lambda c: (c, OFF_Z // SSM_D_INNER)),
            pl.BlockSpec((CHUNK, SSM_D_INNER), lambda c: (c, OFF_XS // SSM_D_INNER)),
            pl.BlockSpec((CHUNK, nst), lambda c: (c, OFF_B // nst)),
            pl.BlockSpec((CHUNK, nst), lambda c: (c, OFF_C // nst)),
            pl.BlockSpec((CHUNK, DT_PAD), lambda c: (c, 0)),
        ] + [whole(a) for a in consts],
        out_specs=pl.BlockSpec((CHUNK, SSM_D_INNER), lambda c: (c, 0)),
        out_shape=jax.ShapeDtypeStruct((t, SSM_D_INNER), BF16),
        scratch_shapes=[
            pltpu.VMEM((SUBLANES, SSM_D_INNER), F32),
            pltpu.VMEM((SUBLANES, nst), F32),
            pltpu.VMEM((SUBLANES, nst), F32),
            pltpu.VMEM((SSM_GROUPS, SSM_STATE, SSM_GROUP_WIDTH), F32),
        ],
        compiler_params=_cparams(("arbitrary",)),
        name="ssd_branch",
    )(proj, proj, proj, proj, dt_raw, *consts)


def _ret_log_gamma(h):
    return float(np.log1p(-np.exp2(-5.0 - h)))


def _rotary(x, cos, sin):
    half = RET_QK_DIM // 2
    x1, x2 = x[:, :half], x[:, half:]
    return jnp.concatenate([x1 * cos - x2 * sin, x2 * cos + x1 * sin], axis=1)


def _ret_kernel(q_ref, k_ref, v_ref, g_ref, pos_ref, freq_ref, o_ref, state_ref):
    @pl.when(pl.program_id(0) == 0)
    def _():
        state_ref[...] = jnp.zeros_like(state_ref)

    ang = pos_ref[...].astype(F32) * freq_ref[...]
    cos, sin = jnp.cos(ang), jnp.sin(ang)
    row = lax.broadcasted_iota(jnp.int32, (CHUNK, CHUNK), 0)
    col = lax.broadcasted_iota(jnp.int32, (CHUNK, CHUNK), 1)
    causal = row >= col
    rel = (row - col).astype(F32)
    r1 = lax.broadcasted_iota(jnp.int32, (CHUNK, 1), 0).astype(F32)
    outs = []
    for h in range(RET_HEADS):
        lg = _ret_log_gamma(h)
        qs = slice(h * RET_QK_DIM, (h + 1) * RET_QK_DIM)
        vs = slice(h * RET_V_DIM, (h + 1) * RET_V_DIM)
        q = _rotary(q_ref[:, qs], cos, sin).astype(BF16)
        k = (_rotary(k_ref[:, qs], cos, sin) * (RET_QK_DIM ** -0.5)).astype(BF16)
        v = v_ref[:, vs]
        scores = lax.dot_general(q, k, (((1,), (1,)), ((), ())), preferred_element_type=F32)
        decay = jnp.exp(jnp.where(causal, rel * lg, -1e30))
        o = jnp.dot((scores * decay).astype(BF16), v.astype(BF16), preferred_element_type=F32)
        prev = state_ref[h]
        o = o + jnp.dot(q, prev.astype(BF16), preferred_element_type=F32) * jnp.exp((r1 + 1.0) * lg)
        v_dec = (v * jnp.exp((CHUNK - 1.0 - r1) * lg)).astype(BF16)
        st = lax.dot_general(k, v_dec, (((0,), (0,)), ((), ())), preferred_element_type=F32)
        state_ref[h] = float(np.exp(CHUNK * lg)) * prev + st
        ms = jnp.mean(o * o, axis=-1, keepdims=True)
        outs.append(o * lax.rsqrt(ms + NORM_EPS) * _silu(g_ref[:, vs]))
    o_ref[...] = jnp.concatenate(outs, axis=1).astype(o_ref.dtype)


def retention_branch(proj, positions):
    t = proj.shape[0]
    half = RET_QK_DIM // 2
    inv_freq = (1.0 / (ROPE_BASE ** (np.arange(half, dtype=np.float32) / half))).astype(np.float32)
    return pl.pallas_call(
        _ret_kernel,
        grid=(t // CHUNK,),
        in_specs=[
            pl.BlockSpec((CHUNK, RET_QK_WIDTH), lambda c: (c, OFF_RQ // RET_QK_WIDTH)),
            pl.BlockSpec((CHUNK, RET_QK_WIDTH), lambda c: (c, OFF_RK // RET_QK_WIDTH)),
            pl.BlockSpec((CHUNK, RET_V_WIDTH), lambda c: (c, OFF_RV // RET_V_WIDTH)),
            pl.BlockSpec((CHUNK, RET_V_WIDTH), lambda c: (c, OFF_RG // RET_V_WIDTH)),
            pl.BlockSpec((CHUNK, 1), lambda c: (c, 0)),
            pl.BlockSpec((1, half), lambda c: (0, 0)),
        ],
        out_specs=pl.BlockSpec((CHUNK, RET_V_WIDTH), lambda c: (c, 0)),
        out_shape=jax.ShapeDtypeStruct((t, RET_V_WIDTH), BF16),
        scratch_shapes=[pltpu.VMEM((RET_HEADS, RET_QK_DIM, RET_V_DIM), F32)],
        compiler_params=_cparams(("arbitrary",)),
        name="retention_branch",
    )(proj, proj, proj, proj, positions.reshape(t, 1), jnp.asarray(inv_freq).reshape(1, half))


def _merge_kernel(y_ref, o_ref, ws_ref, wr_ref, gs_ref, gr_ref, bs_ref, br_ref, m_ref):
    y_ssm = jnp.dot(y_ref[...], ws_ref[...], preferred_element_type=F32)
    y_ret = jnp.dot(o_ref[...], wr_ref[...], preferred_element_type=F32)
    m = _sigmoid(gs_ref[...] + bs_ref[...]) * y_ssm + _sigmoid(gr_ref[...] + br_ref[...]) * y_ret
    m_ref[...] = m.astype(m_ref.dtype)


def gated_merge(y, o, w_ssm_out, w_ret_out, proj, b_gate, *, tm, tn):
    t = y.shape[0]
    nj = D_MODEL // tn
    bg = b_gate.reshape(1, 2 * D_MODEL)
    return pl.pallas_call(
        _merge_kernel,
        grid=(t // tm, nj),
        in_specs=[
            pl.BlockSpec((tm, SSM_D_INNER), lambda i, j: (i, 0)),
            pl.BlockSpec((tm, RET_V_WIDTH), lambda i, j: (i, 0)),
            pl.BlockSpec((SSM_D_INNER, tn), lambda i, j: (0, j)),
            pl.BlockSpec((RET_V_WIDTH, tn), lambda i, j: (0, j)),
            pl.BlockSpec((tm, tn), lambda i, j: (i, OFF_GATE // tn + j)),
            pl.BlockSpec((tm, tn), lambda i, j: (i, OFF_GATE // tn + nj + j)),
            pl.BlockSpec((1, tn), lambda i, j: (0, j)),
            pl.BlockSpec((1, tn), lambda i, j: (0, nj + j)),
        ],
        out_specs=pl.BlockSpec((tm, tn), lambda i, j: (i, j)),
        out_shape=jax.ShapeDtypeStruct((t, D_MODEL), BF16),
        compiler_params=_cparams(("arbitrary", "arbitrary")),
        name="gated_merge",
    )(y, o, w_ssm_out, w_ret_out, proj, proj, bg, bg)


def _out_proj_kernel(m_ref, w_ref, x_ref, o_ref):
    o_ref[...] = x_ref[...] + jnp.dot(m_ref[...], w_ref[...], preferred_element_type=F32)


def out_proj_residual(m, w_o, x, *, tm, tn):
    t = m.shape[0]
    return pl.pallas_call(
        _out_proj_kernel,
        grid=(t // tm, D_MODEL // tn),
        in_specs=[
            pl.BlockSpec((tm, D_MODEL), lambda i, j: (i, 0)),
            pl.BlockSpec((D_MODEL, tn), lambda i, j: (0, j)),
            pl.BlockSpec((tm, tn), lambda i, j: (i, j)),
        ],
        out_specs=pl.BlockSpec((tm, tn), lambda i, j: (i, j)),
        out_shape=jax.ShapeDtypeStruct((t, D_MODEL), F32),
        compiler_params=_cparams(("arbitrary", "arbitrary")),
        name="out_proj_residual",
    )(m, w_o, x)


def _top_k_rows(vals, k, payload=None):
    n, w = vals.shape
    row = lax.broadcasted_iota(jnp.int32, (n, w), 0).astype(F32)
    out_row = lax.broadcasted_iota(jnp.int32, (k, w), 0)
    top_v = jnp.zeros((k, w), F32)
    top_i = jnp.zeros((k, w), F32)
    for i in range(k):
        m = jnp.max(vals, axis=0, keepdims=True)
        sel = jnp.min(jnp.where(vals == m, row, float(n)), axis=0, keepdims=True)
        hit = row == sel
        picked = sel if payload is None else jnp.max(jnp.where(hit, payload, -1.0), axis=0, keepdims=True)
        top_v = jnp.where(out_row == i, m, top_v)
        top_i = jnp.where(out_row == i, picked, top_i)
        vals = jnp.where(hit, -jnp.inf, vals)
    return top_v, top_i


_CAND_PAIRS = [(a, b) for a in range(PEER_TOPK) for b in range(PEER_TOPK) if (a + 1) * (b + 1) <= PEER_TOPK]
_CAND_ROWS = -(-len(_CAND_PAIRS) // 16) * 16


def _candidate_pickers():
    pick = np.zeros((2, _CAND_ROWS, PEER_TOPK), np.float32)
    for r, (a, b) in enumerate(_CAND_PAIRS):
        pick[0, r, a] = 1.0
        pick[1, r, b] = 1.0
    return jnp.asarray(pick, dtype=BF16)


def _route_kernel(q_ref, keys_ref, pick_ref, ids_ref, gates_ref):
    q = q_ref[...]
    tt = q.shape[0]
    pick_a, pick_b = pick_ref[0], pick_ref[1]
    pad = jnp.where(lax.broadcasted_iota(jnp.int32, (_CAND_ROWS, tt), 0) < len(_CAND_PAIRS), 0.0, -jnp.inf)
    ids, gates = [], []
    for h in range(PEER_HEADS):
        q1 = q[:, h * PEER_QUERY_DIM:h * PEER_QUERY_DIM + PEER_HALF]
        q2 = q[:, h * PEER_QUERY_DIM + PEER_HALF:(h + 1) * PEER_QUERY_DIM]
        s1 = lax.dot_general(keys_ref[h, 0], q1, (((1,), (1,)), ((), ())), preferred_element_type=F32)
        s2 = lax.dot_general(keys_ref[h, 1], q2, (((1,), (1,)), ((), ())), preferred_element_type=F32)
        v1, i1 = _top_k_rows(s1, PEER_TOPK)
        v2, i2 = _top_k_rows(s2, PEER_TOPK)
        cand = _dot_left_01(pick_a, v1) + _dot_left_01(pick_b, v2) + pad
        cand_id = (jnp.dot(pick_a, i1.astype(BF16), preferred_element_type=F32) * PEER_N_KEYS
                   + jnp.dot(pick_b, i2.astype(BF16), preferred_element_type=F32))
        top_s, top_id = _top_k_rows(cand, PEER_TOPK, payload=cand_id)
        ids.append(top_id)
        ex = jnp.exp(top_s - top_s[0:1, :])
        gates.append(ex / jnp.sum(ex, axis=0, keepdims=True))
    ids_ref[...] = jnp.concatenate(ids, axis=0).T.astype(jnp.int32)
    gates_ref[...] = jnp.concatenate(gates, axis=0).T


def peer_route(q, sub_keys, *, tt):
    t = q.shape[0]
    pick = _candidate_pickers()
    return pl.pallas_call(
        _route_kernel,
        grid=(t // tt,),
        in_specs=[
            pl.BlockSpec((tt, q.shape[1]), lambda i: (i, 0)),
            pl.BlockSpec(sub_keys.shape, lambda i: (0, 0, 0, 0)),
            pl.BlockSpec(pick.shape, lambda i: (0, 0, 0)),
        ],
        out_specs=[
            pl.BlockSpec((tt, PEER_SEL), lambda i: (i, 0)),
            pl.BlockSpec((tt, PEER_SEL), lambda i: (i, 0)),
        ],
        out_shape=[
            jax.ShapeDtypeStruct((t, PEER_SEL), jnp.int32),
            jax.ShapeDtypeStruct((t, PEER_SEL), F32),
        ],
        compiler_params=_cparams(("arbitrary",)),
        name="peer_route",
    )(q, sub_keys, pick)


ROW_TILES = D_MODEL // LANES
PEER_TB = 8
PEER_ROWS = PEER_TB * PEER_SEL
PEER_KC = 16
PEER_NBUF = 4
PEER_AHEAD = 2


def _erf(x):
    x = jnp.clip(x, -4.0, 4.0)
    x2 = x * x
    p = -2.72614225801306e-10
    for c in (2.77068142495902e-08, -2.10102402082508e-06, -5.69250639462346e-05,
              -7.34990630326855e-04, -2.95459980854025e-03, -1.60960333262415e-02):
        p = p * x2 + c
    q = -1.45660718464996e-05
    for c in (-2.13374055278905e-04, -1.68282697438203e-03, -7.37332916720468e-03, -1.42647390514189e-02):
        q = q * x2 + c
    return x * p / q


def _gelu_exact(x):
    return 0.5 * x * (1.0 + _erf(x * float(1.0 / np.sqrt(2.0))))


class _Prefetch:
    def __init__(self, table_ref, ids_ref, ids_row0, buf, sem):
        self.table_ref, self.ids_ref, self.ids_row0, self.buf, self.sem = table_ref, ids_ref, ids_row0, buf, sem

    def issue(self, t, k0, k1):
        for k in range(k0, k1):
            e = self.ids_ref[self.ids_row0 + t, k]
            pltpu.make_async_copy(self.table_ref.at[e], self.buf.at[t * PEER_SEL + k], self.sem).start(priority=k % 2)

    def issue_all(self):
        def body(t, carry):
            for k in range(PEER_SEL):
                e = self.ids_ref[self.ids_row0 + t, k]
                pltpu.make_async_copy(self.table_ref.at[e], self.buf.at[t * PEER_SEL + k], self.sem).start(priority=k % 2)
            return carry
        lax.fori_loop(0, PEER_TB, body, 0)

    def wait(self):
        pltpu.make_async_copy(self.table_ref.at[pl.ds(0, PEER_ROWS)], self.buf, self.sem).wait()


def _peer_compute(buf, h_ref, g_ref, x_ref, o_ref, row0, ones_ref, onehot_ref, nxt):
    ones = ones_ref[...]
    half = PEER_SEL // 2
    s_rows = []
    for t in range(PEER_TB):
        h = h_ref[row0 + t]
        z = jnp.zeros((SUBLANES, LANES), F32)
        for c in range(PEER_SEL // PEER_KC):
            base = t * PEER_SEL + c * PEER_KC
            u = buf[base:base + PEER_KC, 0:ROW_TILES, :].astype(F32)
            p = u * h[None]
            p8 = p[:, :SUBLANES] + p[:, SUBLANES:]
            r = jnp.dot(p8.reshape(PEER_KC * SUBLANES, LANES), ones, preferred_element_type=F32)
            r = r.reshape(PEER_KC, SUBLANES, LANES)
            z = z + jnp.sum(r * onehot_ref[c * PEER_KC:(c + 1) * PEER_KC], axis=0)
        s_rows.append(jnp.sum(z, axis=0, keepdims=True))
        nxt.issue(t, 0, half)
    s = jnp.concatenate(s_rows, axis=0)
    w = g_ref[pl.ds(row0, PEER_TB), :] * _gelu_exact(s)
    for t in range(PEER_TB):
        w_t = w[t:t + 1, :]
        acc = jnp.zeros((2, SUBLANES, LANES), F32)
        for c in range(PEER_SEL // PEER_KC):
            base = t * PEER_SEL + c * PEER_KC
            a = onehot_ref[c * PEER_KC:(c + 1) * PEER_KC] * w_t[None]
            wsp = jnp.dot(a.reshape(PEER_KC * SUBLANES, LANES), ones,
                          preferred_element_type=F32).reshape(PEER_KC, 1, SUBLANES, LANES)
            v = buf[base:base + PEER_KC, ROW_TILES:2 * ROW_TILES, :].astype(F32)
            acc = acc + jnp.sum(v.reshape(PEER_KC, 2, SUBLANES, LANES) * wsp, axis=0)
        o_ref[row0 + t] = x_ref[row0 + t] + acc.reshape(ROW_TILES, LANES)
        nxt.issue(t, half, PEER_SEL)


def _peer_kernel(ids_cur, ids_nxt, table_ref, h_ref, g_ref, x_ref, ones_ref, onehot_ref, o_ref, *scratch):
    bufs, sems = scratch[:PEER_NBUF], scratch[PEER_NBUF:]
    i = pl.program_id(0)
    cur = [_Prefetch(table_ref, ids_cur, b * PEER_TB, bufs[b], sems[b]) for b in range(PEER_NBUF)]
    nxt = [_Prefetch(table_ref, ids_nxt, b * PEER_TB, bufs[b], sems[b]) for b in range(PEER_AHEAD)]

    @pl.when(i == 0)
    def _():
        for b in range(PEER_AHEAD):
            cur[b].issue_all()

    for b in range(PEER_NBUF):
        cur[b].wait()
        ahead = cur[b + PEER_AHEAD] if b + PEER_AHEAD < PEER_NBUF else nxt[b + PEER_AHEAD - PEER_NBUF]
        _peer_compute(bufs[b], h_ref, g_ref, x_ref, o_ref, b * PEER_TB, ones_ref, onehot_ref, ahead)

    @pl.when(i == pl.num_programs(0) - 1)
    def _():
        for b in range(PEER_AHEAD):
            nxt[b].wait()


def peer_experts(ids, gates, table, h, x):
    t = h.shape[0]
    tb2 = PEER_NBUF * PEER_TB
    nblk = t // tb2
    h3 = h.reshape(t, ROW_TILES, LANES)
    x3 = x.reshape(t, ROW_TILES, LANES)
    ones = jnp.ones((LANES, LANES), F32)
    onehot = jnp.asarray(np.broadcast_to(np.eye(PEER_SEL, LANES, dtype=np.float32)[:, None, :],
                                         (PEER_SEL, SUBLANES, LANES)))
    out = pl.pallas_call(
        _peer_kernel,
        grid=(nblk,),
        in_specs=[
            pl.BlockSpec((tb2, PEER_SEL), lambda i: (i, 0), memory_space=pltpu.SMEM),
            pl.BlockSpec((tb2, PEER_SEL), lambda i: (jnp.minimum(i + 1, nblk - 1), 0), memory_space=pltpu.SMEM),
            pl.BlockSpec(memory_space=pl.ANY),
            pl.BlockSpec((tb2, ROW_TILES, LANES), lambda i: (i, 0, 0)),
            pl.BlockSpec((tb2, PEER_SEL), lambda i: (i, 0)),
            pl.BlockSpec((tb2, ROW_TILES, LANES), lambda i: (i, 0, 0)),
            pl.BlockSpec((LANES, LANES), lambda i: (0, 0)),
            pl.BlockSpec((PEER_SEL, SUBLANES, LANES), lambda i: (0, 0, 0)),
        ],
        out_specs=pl.BlockSpec((tb2, ROW_TILES, LANES), lambda i: (i, 0, 0)),
        out_shape=jax.ShapeDtypeStruct((t, ROW_TILES, LANES), F32),
        scratch_shapes=([pltpu.VMEM((PEER_ROWS, 2 * ROW_TILES, LANES), table.dtype)] * PEER_NBUF
                        + [pltpu.SemaphoreType.DMA] * PEER_NBUF),
        compiler_params=_cparams(("arbitrary",)),
        name="peer_experts",
    )(ids, ids, table, h3, gates, x3, ones, onehot)
    return out.reshape(t, D_MODEL)


def _final_norm_kernel(x_ref, w_ref, o_ref):
    x = x_ref[...]
    ms = jnp.mean(x * x, axis=-1, keepdims=True)
    o_ref[...] = x * lax.rsqrt(ms + NORM_EPS) * w_ref[...]


def final_norm(x, w, *, tm):
    t, d = x.shape
    return pl.pallas_call(
        _final_norm_kernel,
        grid=(t // tm,),
        in_specs=[pl.BlockSpec((tm, d), lambda i: (i, 0)), pl.BlockSpec((1, d), lambda i: (0, 0))],
        out_specs=pl.BlockSpec((tm, d), lambda i: (i, 0)),
        out_shape=jax.ShapeDtypeStruct((t, d), F32),
        compiler_params=_cparams(("arbitrary",)),
        name="final_norm",
    )(x, w.reshape(1, d))


def _split_w_in(w_in):
    w_ssm = w_in[:, :SSM_PART].astype(BF16)
    w_dt = jnp.pad(w_in[:, SSM_PART:SSM_PART + SSM_HEADS], ((0, 0), (0, DT_PAD - SSM_HEADS))).astype(BF16)
    w_ret = w_in[:, SSM_PART + SSM_HEADS:].astype(BF16)
    return w_ssm, w_dt, w_ret


def _mixer_layer(x, positions, norm_mix, w_in, conv_w, conv_b, dt_bias, a_log, d_skip, ssm_norm,
                 w_ssm_out, w_ret_out, b_gate, w_o):
    w_ssm, w_dt, w_ret = _split_w_in(w_in)
    proj_ssm = norm_matmul(x, norm_mix, w_ssm, tm=1024, tn=1024)
    dt_raw = norm_matmul(x, norm_mix, w_dt, tm=1024, tn=DT_PAD)
    proj_ret = norm_matmul(x, norm_mix, w_ret, tm=1024, tn=1024)
    y = ssd_branch(proj_ssm, dt_raw, conv_w, conv_b, dt_bias, a_log, d_skip, ssm_norm)
    o = retention_branch(proj_ret, positions)
    m = gated_merge(y, o, w_ssm_out.astype(BF16), w_ret_out.astype(BF16), proj_ret, b_gate, tm=512, tn=512)
    return out_proj_residual(m, w_o.astype(BF16), x, tm=1024, tn=1024)


def _peer_layer(x, norm_ffn, w_query, sub_keys, expert_u, expert_v):
    q, h = norm_matmul(x, norm_ffn, w_query.astype(BF16), tm=1024, tn=1024, emit_h=True)
    ids, gates = peer_route(q, sub_keys, tt=256)
    table = jnp.concatenate([expert_u.astype(BF16).reshape(PEER_EXPERTS, ROW_TILES, LANES),
                             expert_v.astype(BF16).reshape(PEER_EXPERTS, ROW_TILES, LANES)], axis=1)
    return peer_experts(ids, gates, table, h, x)


def kernel(x, positions, norm_mix, w_in, conv_w, conv_b, dt_bias, a_log, d_skip, ssm_norm, w_ssm_out, w_ret_out,
           b_gate, w_o, norm_ffn, w_query, sub_keys, expert_u, expert_v, norm_final):
    b, t, d = x.shape
    depth = norm_mix.shape[0]
    outs = []
    for bi in range(b):
        xb = x[bi]
        pos = positions[bi]
        for l in range(depth):
            xb = _mixer_layer(xb, pos, norm_mix[l], w_in[l], conv_w[l], conv_b[l], dt_bias[l], a_log[l], d_skip[l],
                              ssm_norm[l], w_ssm_out[l], w_ret_out[l], b_gate[l], w_o[l])
            xb = _peer_layer(xb, norm_ffn[l], w_query[l], sub_keys[l], expert_u[l], expert_v[l])
        outs.append(final_norm(xb, norm_final, tm=512))
    return jnp.stack(outs, axis=0)
```

```python
import functools

import jax
import jax.numpy as jnp
import numpy as np
from jax import lax
from jax.experimental import pallas as pl
from jax.experimental.pallas import tpu as pltpu

F32 = jnp.float32
BF16 = jnp.bfloat16

D_MODEL = 2048
CHUNK = 128
SSM_D_INNER = 2 * D_MODEL
SSM_HEAD_DIM = 64
SSM_HEADS = SSM_D_INNER // SSM_HEAD_DIM
SSM_GROUPS = 8
SSM_HPG = SSM_HEADS // SSM_GROUPS
SSM_STATE = 128
SSM_GROUP_WIDTH = SSM_D_INNER // SSM_GROUPS
CONV_WIDTH = 4
RET_HEADS = 8
RET_QK_DIM = D_MODEL // RET_HEADS
RET_V_DIM = 2 * RET_QK_DIM
RET_QK_WIDTH = RET_HEADS * RET_QK_DIM
RET_V_WIDTH = RET_HEADS * RET_V_DIM
ROPE_BASE = 10000.0
PEER_HEADS = 8
PEER_N_KEYS = 128
PEER_EXPERTS = PEER_N_KEYS * PEER_N_KEYS
PEER_TOPK = 16
PEER_QUERY_DIM = 256
PEER_HALF = PEER_QUERY_DIM // 2
PEER_SEL = PEER_HEADS * PEER_TOPK
NORM_EPS = 1e-6

LANES = 128
SUBLANES = 8
VMEM_LIMIT_BYTES = 56 * 1024 * 1024

SSM_NST = SSM_GROUPS * SSM_STATE
OFF_Z = 0
OFF_XS = OFF_Z + SSM_D_INNER
OFF_B = OFF_XS + SSM_D_INNER
OFF_C = OFF_B + SSM_NST
SSM_PART = OFF_C + SSM_NST
DT_PAD = LANES
OFF_RQ = 0
OFF_RK = OFF_RQ + RET_QK_WIDTH
OFF_RV = OFF_RK + RET_QK_WIDTH
OFF_RG = OFF_RV + RET_V_WIDTH
OFF_GATE = OFF_RG + RET_V_WIDTH
RET_PART = OFF_GATE + 2 * D_MODEL

def _cparams(semantics):
    return pltpu.CompilerParams(dimension_semantics=semantics, vmem_limit_bytes=VMEM_LIMIT_BYTES)


def _silu(v):
    return v * (1.0 / (1.0 + jnp.exp(-v)))


def _sigmoid(v):
    return 1.0 / (1.0 + jnp.exp(-v))


def _softplus(v):
    return jnp.maximum(v, 0.0) + jnp.log(1.0 + jnp.exp(-jnp.abs(v)))


def _norm_matmul_kernel(x_ref, nw_ref, w_ref, o_ref, *rest, emit_h):
    if emit_h:
        h_out_ref, h_ref = rest
    else:
        (h_ref,) = rest

    @pl.when(pl.program_id(1) == 0)
    def _():
        x = x_ref[...]
        ms = jnp.mean(x * x, axis=-1, keepdims=True)
        h = x * lax.rsqrt(ms + NORM_EPS) * nw_ref[...]
        h_ref[...] = h.astype(BF16)
        if emit_h:
            h_out_ref[...] = h

    o_ref[...] = jnp.dot(h_ref[...], w_ref[...], preferred_element_type=F32)


def norm_matmul(x, nw, w, *, tm, tn, emit_h=False):
    t, d = x.shape
    n = w.shape[1]
    out_shape = [jax.ShapeDtypeStruct((t, n), F32)]
    out_specs = [pl.BlockSpec((tm, tn), lambda i, j: (i, j))]
    if emit_h:
        out_shape.append(jax.ShapeDtypeStruct((t, d), F32))
        out_specs.append(pl.BlockSpec((tm, d), lambda i, j: (i, 0)))
    res = pl.pallas_call(
        functools.partial(_norm_matmul_kernel, emit_h=emit_h),
        grid=(t // tm, n // tn),
        in_specs=[
            pl.BlockSpec((tm, d), lambda i, j: (i, 0)),
            pl.BlockSpec((1, d), lambda i, j: (0, 0)),
            pl.BlockSpec((d, tn), lambda i, j: (0, j)),
        ],
        out_specs=out_specs,
        out_shape=out_shape,
        scratch_shapes=[pltpu.VMEM((tm, d), BF16)],
        compiler_params=_cparams(("arbitrary", "arbitrary")),
        name="norm_matmul",
    )(x, nw.reshape(1, d), w)
    return res if emit_h else res[0]


def _causal_conv_silu(x, tail_ref, w_ref, b_ref):
    tail = tail_ref[...]
    w = w_ref[...]
    row8 = lax.broadcasted_iota(jnp.int32, (SUBLANES, x.shape[1]), 0)
    acc = x * w[CONV_WIDTH - 1:CONV_WIDTH, :] + b_ref[...]
    for s in range(1, CONV_WIDTH):
        rolled = pltpu.roll(x, s, axis=0)
        head = jnp.where(row8 < s, pltpu.roll(tail, s, axis=0), rolled[:SUBLANES])
        shifted = jnp.concatenate([head, rolled[SUBLANES:]], axis=0)
        acc = acc + shifted * w[CONV_WIDTH - 1 - s:CONV_WIDTH - s, :]
    tail_ref[...] = x[CHUNK - SUBLANES:]
    return _silu(acc)


def _split3(a):
    hi = a.astype(BF16)
    r1 = a - hi.astype(F32)
    mid = r1.astype(BF16)
    lo = (r1 - mid.astype(F32)).astype(BF16)
    return hi, mid, lo


def _dot_right_01(a, m01):
    return sum(jnp.dot(part, m01, preferred_element_type=F32) for part in _split3(a))


def _dot_left_01(m01, a):
    return sum(jnp.dot(m01, part, preferred_element_type=F32) for part in _split3(a))


def _ssd_kernel(z_ref, xs_ref, b_ref, c_ref, dt_ref,
                cw_xs_ref, cb_xs_ref, cw_b_ref, cb_b_ref, cw_c_ref, cb_c_ref,
                dtb_ref, alog_ref, dskip_ref, nw_ref, expand_ref, wout_ref,
                y_ref,
                tail_xs, tail_b, tail_c, state_ref):
    @pl.when(pl.program_id(0) == 0)
    def _():
        tail_xs[...] = jnp.zeros_like(tail_xs)
        tail_b[...] = jnp.zeros_like(tail_b)
        tail_c[...] = jnp.zeros_like(tail_c)
        state_ref[...] = jnp.zeros_like(state_ref)

    xs = _causal_conv_silu(xs_ref[...], tail_xs, cw_xs_ref, cb_xs_ref)
    bm = _causal_conv_silu(b_ref[...], tail_b, cw_b_ref, cb_b_ref)
    cm = _causal_conv_silu(c_ref[...], tail_c, cw_c_ref, cb_c_ref)

    dt = _softplus(dt_ref[...] + dtb_ref[...])
    la = dt * (-jnp.exp(alog_ref[...]))
    row = lax.broadcasted_iota(jnp.int32, (CHUNK, CHUNK), 0)
    col = lax.broadcasted_iota(jnp.int32, (CHUNK, CHUNK), 1)
    causal = row >= col
    cum = _dot_left_01(causal.astype(BF16), la)
    cum_t = cum.T
    total = cum[CHUNK - 1:CHUNK, :]

    expand = expand_ref[...]
    dt_x = _dot_right_01(dt, expand)
    ecum_x = _dot_right_01(jnp.exp(cum), expand)
    edec_x = _dot_right_01(jnp.exp(total - cum), expand)
    xdt = xs * dt_x
    v_dec = (xdt * edec_x).astype(BF16)
    xdt_b = xdt.astype(BF16)
    etot_x = ecum_x[CHUNK - 1:CHUNK, :]

    lane = lax.broadcasted_iota(jnp.int32, (CHUNK, 2 * SSM_HEAD_DIM), 1)
    left = lane < SSM_HEAD_DIM
    y_out = jnp.zeros((CHUNK, D_MODEL), F32)
    for g in range(SSM_GROUPS):
        ns = slice(g * SSM_STATE, (g + 1) * SSM_STATE)
        gs = slice(g * SSM_GROUP_WIDTH, (g + 1) * SSM_GROUP_WIDTH)
        c_g = cm[:, ns].astype(BF16)
        b_g = bm[:, ns].astype(BF16)
        scores = lax.dot_general(c_g, b_g, (((1,), (1,)), ((), ())), preferred_element_type=F32)
        prev = state_ref[g]
        y_inter = jnp.dot(c_g, prev.astype(BF16), preferred_element_type=F32) * ecum_x[:, gs]
        intra = []
        for pr in range(SSM_HPG // 2):
            cs = slice(g * SSM_GROUP_WIDTH + pr * 2 * SSM_HEAD_DIM, g * SSM_GROUP_WIDTH + (pr + 1) * 2 * SSM_HEAD_DIM)
            v_pair = xdt_b[:, cs]
            outs = []
            for sub in range(2):
                hd = g * SSM_HPG + pr * 2 + sub
                seg = cum[:, hd:hd + 1] - cum_t[hd:hd + 1, :]
                decay = jnp.exp(jnp.where(causal, seg, -1e30))
                m = (scores * decay).astype(BF16)
                outs.append(jnp.dot(m, v_pair, preferred_element_type=F32))
            intra.append(jnp.where(left, outs[0], outs[1]))
        y_g = jnp.concatenate(intra, axis=1) + y_inter
        st = lax.dot_general(b_g, v_dec[:, gs], (((0,), (0,)), ((), ())), preferred_element_type=F32)
        state_ref[g] = etot_x[:, gs] * prev + st
        y_g = y_g + xs[:, gs] * dskip_ref[:, gs]
        y_g = y_g * _silu(z_ref[:, gs])
        ms = jnp.mean(y_g * y_g, axis=-1, keepdims=True)
        y_g = (y_g * lax.rsqrt(ms + NORM_EPS) * nw_ref[:, gs]).astype(BF16)
        y_out = y_out + jnp.dot(y_g, wout_ref[gs, :], preferred_element_type=F32)
    y_ref[...] = y_out


def _resident(a):
    return pl.BlockSpec(a.shape, lambda c: (0,) * a.ndim, pipeline_mode=pl.Buffered(1))


def ssd_branch(proj, dt_raw, cw, cb, dt_bias, a_log, d_skip, ssm_norm, w_out):
    t = proj.shape[0]
    nst = SSM_GROUPS * SSM_STATE
    cw_xs, cw_b, cw_c = cw[:, :SSM_D_INNER], cw[:, SSM_D_INNER:SSM_D_INNER + nst], cw[:, SSM_D_INNER + nst:]
    cb = cb.reshape(1, -1)
    cb_xs, cb_b, cb_c = cb[:, :SSM_D_INNER], cb[:, SSM_D_INNER:SSM_D_INNER + nst], cb[:, SSM_D_INNER + nst:]
    pad = DT_PAD - SSM_HEADS
    dtb = jnp.pad(dt_bias, (0, pad)).reshape(1, DT_PAD)
    alog = jnp.pad(a_log, (0, pad)).reshape(1, DT_PAD)
    dskip = jnp.repeat(d_skip, SSM_HEAD_DIM).reshape(1, SSM_D_INNER)
    expand = (np.arange(DT_PAD)[:, None] == (np.arange(SSM_D_INNER)[None, :] // SSM_HEAD_DIM)).astype(np.float32)

    def whole(a):
        return pl.BlockSpec(a.shape, lambda c: (0,) * a.ndim)

    consts = [cw_xs, cb_xs, cw_b, cb_b, cw_c, cb_c, dtb, alog, dskip, ssm_norm.reshape(1, -1),
              jnp.asarray(expand, dtype=BF16)]
    return pl.pallas_call(
        _ssd_kernel,
        grid=(t // CHUNK,),
        in_specs=[
            pl.BlockSpec((CHUNK, SSM_D_INNER), lambda c: (c, OFF_Z // SSM_D_INNER)),
            pl.BlockSpec((CHUNK, SSM_D_INNER), lambda c: (c, OFF_XS // SSM_D_INNER)),
            pl.BlockSpec((CHUNK, nst), lambda c: (c, OFF_B // nst)),
            pl.BlockSpec((CHUNK, nst), lambda c: (c, OFF_C // nst)),
            pl.BlockSpec((CHUNK, DT_PAD), lambda c: (c, 0)),
        ] + [whole(a) for a in consts] + [_resident(w_out)],
        out_specs=pl.BlockSpec((CHUNK, D_MODEL), lambda c: (c, 0)),
        out_shape=jax.ShapeDtypeStruct((t, D_MODEL), F32),
        scratch_shapes=[
            pltpu.VMEM((SUBLANES, SSM_D_INNER), F32),
            pltpu.VMEM((SUBLANES, nst), F32),
            pltpu.VMEM((SUBLANES, nst), F32),
            pltpu.VMEM((SSM_GROUPS, SSM_STATE, SSM_GROUP_WIDTH), F32),
        ],
        compiler_params=_cparams(("arbitrary",)),
        name="ssd_branch",
    )(proj, proj, proj, proj, dt_raw, *consts, w_out)


def _ret_log_gamma(h):
    return float(np.log1p(-np.exp2(-5.0 - h)))


def _rotary(x, cos, sin):
    half = RET_QK_DIM // 2
    x1, x2 = x[:, :half], x[:, half:]
    return jnp.concatenate([x1 * cos - x2 * sin, x2 * cos + x1 * sin], axis=1)


def _ret_kernel(q_ref, k_ref, v_ref, g_ref, pos_ref, freq_ref, wout_ref, o_ref, state_ref):
    @pl.when(pl.program_id(0) == 0)
    def _():
        state_ref[...] = jnp.zeros_like(state_ref)

    ang = pos_ref[...].astype(F32) * freq_ref[...]
    cos, sin = jnp.cos(ang), jnp.sin(ang)
    row = lax.broadcasted_iota(jnp.int32, (CHUNK, CHUNK), 0)
    col = lax.broadcasted_iota(jnp.int32, (CHUNK, CHUNK), 1)
    causal = row >= col
    rel = (row - col).astype(F32)
    r1 = lax.broadcasted_iota(jnp.int32, (CHUNK, 1), 0).astype(F32)
    out = jnp.zeros((CHUNK, D_MODEL), F32)
    for h in range(RET_HEADS):
        lg = _ret_log_gamma(h)
        qs = slice(h * RET_QK_DIM, (h + 1) * RET_QK_DIM)
        vs = slice(h * RET_V_DIM, (h + 1) * RET_V_DIM)
        q = _rotary(q_ref[:, qs], cos, sin).astype(BF16)
        k = (_rotary(k_ref[:, qs], cos, sin) * (RET_QK_DIM ** -0.5)).astype(BF16)
        v = v_ref[:, vs]
        scores = lax.dot_general(q, k, (((1,), (1,)), ((), ())), preferred_element_type=F32)
        decay = jnp.exp(jnp.where(causal, rel * lg, -1e30))
        o = jnp.dot((scores * decay).astype(BF16), v.astype(BF16), preferred_element_type=F32)
        prev = state_ref[h]
        o = o + jnp.dot(q, prev.astype(BF16), preferred_element_type=F32) * jnp.exp((r1 + 1.0) * lg)
        v_dec = (v * jnp.exp((CHUNK - 1.0 - r1) * lg)).astype(BF16)
        st = lax.dot_general(k, v_dec, (((0,), (0,)), ((), ())), preferred_element_type=F32)
        state_ref[h] = float(np.exp(CHUNK * lg)) * prev + st
        ms = jnp.mean(o * o, axis=-1, keepdims=True)
        o = (o * lax.rsqrt(ms + NORM_EPS) * _silu(g_ref[:, vs])).astype(BF16)
        out = out + jnp.dot(o, wout_ref[vs, :], preferred_element_type=F32)
    o_ref[...] = out


def retention_branch(proj, positions, w_out):
    t = proj.shape[0]
    half = RET_QK_DIM // 2
    inv_freq = (1.0 / (ROPE_BASE ** (np.arange(half, dtype=np.float32) / half))).astype(np.float32)
    return pl.pallas_call(
        _ret_kernel,
        grid=(t // CHUNK,),
        in_specs=[
            pl.BlockSpec((CHUNK, RET_QK_WIDTH), lambda c: (c, OFF_RQ // RET_QK_WIDTH)),
            pl.BlockSpec((CHUNK, RET_QK_WIDTH), lambda c: (c, OFF_RK // RET_QK_WIDTH)),
            pl.BlockSpec((CHUNK, RET_V_WIDTH), lambda c: (c, OFF_RV // RET_V_WIDTH)),
            pl.BlockSpec((CHUNK, RET_V_WIDTH), lambda c: (c, OFF_RG // RET_V_WIDTH)),
            pl.BlockSpec((CHUNK, 1), lambda c: (c, 0)),
            pl.BlockSpec((1, half), lambda c: (0, 0)),
            _resident(w_out),
        ],
        out_specs=pl.BlockSpec((CHUNK, D_MODEL), lambda c: (c, 0)),
        out_shape=jax.ShapeDtypeStruct((t, D_MODEL), F32),
        scratch_shapes=[pltpu.VMEM((RET_HEADS, RET_QK_DIM, RET_V_DIM), F32)],
        compiler_params=_cparams(("arbitrary",)),
        name="retention_branch",
    )(proj, proj, proj, proj, positions.reshape(t, 1), jnp.asarray(inv_freq).reshape(1, half), w_out)


def _merge_out_kernel(ys_ref, yr_ref, gs_ref, gr_ref, bs_ref, br_ref, w_ref, x_ref, o_ref):
    m = _sigmoid(gs_ref[...] + bs_ref[...]) * ys_ref[...] + _sigmoid(gr_ref[...] + br_ref[...]) * yr_ref[...]
    o_ref[...] = x_ref[...] + jnp.dot(m.astype(BF16), w_ref[...], preferred_element_type=F32)


def merge_out_proj(y_ssm, y_ret, proj_ret, b_gate, w_o, x, *, tm):
    t = x.shape[0]
    bg = b_gate.reshape(1, 2 * D_MODEL)
    gate0 = OFF_GATE // D_MODEL
    row_tile = pl.BlockSpec((tm, D_MODEL), lambda i: (i, 0))
    return pl.pallas_call(
        _merge_out_kernel,
        grid=(t // tm,),
        in_specs=[
            row_tile,
            row_tile,
            pl.BlockSpec((tm, D_MODEL), lambda i: (i, gate0)),
            pl.BlockSpec((tm, D_MODEL), lambda i: (i, gate0 + 1)),
            pl.BlockSpec((1, D_MODEL), lambda i: (0, 0)),
            pl.BlockSpec((1, D_MODEL), lambda i: (0, 1)),
            _resident(w_o),
            row_tile,
        ],
        out_specs=row_tile,
        out_shape=jax.ShapeDtypeStruct((t, D_MODEL), F32),
        compiler_params=_cparams(("arbitrary",)),
        name="merge_out_proj",
    )(y_ssm, y_ret, proj_ret, proj_ret, bg, bg, w_o, x)


def _top_k_rows(vals, k, payload=None):
    n, w = vals.shape
    row = lax.broadcasted_iota(jnp.int32, (n, w), 0).astype(F32)
    out_row = lax.broadcasted_iota(jnp.int32, (k, w), 0)
    top_v = jnp.zeros((k, w), F32)
    top_i = jnp.zeros((k, w), F32)
    for i in range(k):
        m = jnp.max(vals, axis=0, keepdims=True)
        sel = jnp.min(jnp.where(vals == m, row, float(n)), axis=0, keepdims=True)
        hit = row == sel
        picked = sel if payload is None else jnp.max(jnp.where(hit, payload, -1.0), axis=0, keepdims=True)
        top_v = jnp.where(out_row == i, m, top_v)
        top_i = jnp.where(out_row == i, picked, top_i)
        vals = jnp.where(hit, -jnp.inf, vals)
    return top_v, top_i


_CAND_PAIRS = [(a, b) for a in range(PEER_TOPK) for b in range(PEER_TOPK) if (a + 1) * (b + 1) <= PEER_TOPK]
_CAND_ROWS = -(-len(_CAND_PAIRS) // 16) * 16


def _candidate_pickers():
    pick = np.zeros((2, _CAND_ROWS, PEER_TOPK), np.float32)
    for r, (a, b) in enumerate(_CAND_PAIRS):
        pick[0, r, a] = 1.0
        pick[1, r, b] = 1.0
    return jnp.asarray(pick, dtype=BF16)


def _route_kernel(q_ref, keys_ref, pick_ref, ids_ref, gates_ref):
    q = q_ref[...]
    tt = q.shape[0]
    pick_a, pick_b = pick_ref[0], pick_ref[1]
    pad = jnp.where(lax.broadcasted_iota(jnp.int32, (_CAND_ROWS, tt), 0) < len(_CAND_PAIRS), 0.0, -jnp.inf)
    ids, gates = [], []
    for h in range(PEER_HEADS):
        q1 = q[:, h * PEER_QUERY_DIM:h * PEER_QUERY_DIM + PEER_HALF]
        q2 = q[:, h * PEER_QUERY_DIM + PEER_HALF:(h + 1) * PEER_QUERY_DIM]
        s1 = lax.dot_general(keys_ref[h, 0], q1, (((1,), (1,)), ((), ())), preferred_element_type=F32)
        s2 = lax.dot_general(keys_ref[h, 1], q2, (((1,), (1,)), ((), ())), preferred_element_type=F32)
        v1, i1 = _top_k_rows(s1, PEER_TOPK)
        v2, i2 = _top_k_rows(s2, PEER_TOPK)
        cand = _dot_left_01(pick_a, v1) + _dot_left_01(pick_b, v2) + pad
        cand_id = (jnp.dot(pick_a, i1.astype(BF16), preferred_element_type=F32) * PEER_N_KEYS
                   + jnp.dot(pick_b, i2.astype(BF16), preferred_element_type=F32))
        top_s, top_id = _top_k_rows(cand, PEER_TOPK, payload=cand_id)
        ids.append(top_id)
        ex = jnp.exp(top_s - top_s[0:1, :])
        gates.append(ex / jnp.sum(ex, axis=0, keepdims=True))
    ids_ref[...] = jnp.concatenate(ids, axis=0).T.astype(jnp.int32)
    gates_ref[...] = jnp.concatenate(gates, axis=0).T


def peer_route(q, sub_keys, *, tt):
    t = q.shape[0]
    pick = _candidate_pickers()
    return pl.pallas_call(
        _route_kernel,
        grid=(t // tt,),
        in_specs=[
            pl.BlockSpec((tt, q.shape[1]), lambda i: (i, 0)),
            pl.BlockSpec(sub_keys.shape, lambda i: (0, 0, 0, 0)),
            pl.BlockSpec(pick.shape, lambda i: (0, 0, 0)),
        ],
        out_specs=[
            pl.BlockSpec((tt, PEER_SEL), lambda i: (i, 0)),
            pl.BlockSpec((tt, PEER_SEL), lambda i: (i, 0)),
        ],
        out_shape=[
            jax.ShapeDtypeStruct((t, PEER_SEL), jnp.int32),
            jax.ShapeDtypeStruct((t, PEER_SEL), F32),
        ],
        compiler_params=_cparams(("arbitrary",)),
        name="peer_route",
    )(q, sub_keys, pick)


ROW_TILES = D_MODEL // LANES
PEER_TB = 8
PEER_ROWS = PEER_TB * PEER_SEL
PEER_KC = 16
PEER_NBUF = 4
PEER_AHEAD = 2


def _erf(x):
    x = jnp.clip(x, -4.0, 4.0)
    x2 = x * x
    p = -2.72614225801306e-10
    for c in (2.77068142495902e-08, -2.10102402082508e-06, -5.69250639462346e-05,
              -7.34990630326855e-04, -2.95459980854025e-03, -1.60960333262415e-02):
        p = p * x2 + c
    q = -1.45660718464996e-05
    for c in (-2.13374055278905e-04, -1.68282697438203e-03, -7.37332916720468e-03, -1.42647390514189e-02):
        q = q * x2 + c
    return x * p / q


def _gelu_exact(x):
    return 0.5 * x * (1.0 + _erf(x * float(1.0 / np.sqrt(2.0))))


class _Prefetch:
    def __init__(self, table_ref, ids_ref, ids_row0, buf, sem):
        self.table_ref, self.ids_ref, self.ids_row0, self.buf, self.sem = table_ref, ids_ref, ids_row0, buf, sem

    def issue(self, t, k0, k1):
        for k in range(k0, k1):
            e = self.ids_ref[self.ids_row0 + t, k]
            pltpu.make_async_copy(self.table_ref.at[e], self.buf.at[t * PEER_SEL + k], self.sem).start(priority=k % 2)

    def issue_all(self):
        def body(t, carry):
            for k in range(PEER_SEL):
                e = self.ids_ref[self.ids_row0 + t, k]
                pltpu.make_async_copy(self.table_ref.at[e], self.buf.at[t * PEER_SEL + k], self.sem).start(priority=k % 2)
            return carry
        lax.fori_loop(0, PEER_TB, body, 0)

    def wait(self):
        pltpu.make_async_copy(self.table_ref.at[pl.ds(0, PEER_ROWS)], self.buf, self.sem).wait()


def _peer_compute(buf, h_ref, g_ref, x_ref, o_ref, row0, ones_ref, onehot_ref, nxt):
    ones = ones_ref[...]
    half = PEER_SEL // 2
    s_rows = []
    for t in range(PEER_TB):
        h = h_ref[row0 + t]
        z = jnp.zeros((SUBLANES, LANES), F32)
        for c in range(PEER_SEL // PEER_KC):
            base = t * PEER_SEL + c * PEER_KC
            u = buf[base:base + PEER_KC, 0:ROW_TILES, :].astype(F32)
            p = u * h[None]
            p8 = p[:, :SUBLANES] + p[:, SUBLANES:]
            r = jnp.dot(p8.reshape(PEER_KC * SUBLANES, LANES), ones, preferred_element_type=F32)
            r = r.reshape(PEER_KC, SUBLANES, LANES)
            z = z + jnp.sum(r * onehot_ref[c * PEER_KC:(c + 1) * PEER_KC], axis=0)
        s_rows.append(jnp.sum(z, axis=0, keepdims=True))
        nxt.issue(t, 0, half)
    s = jnp.concatenate(s_rows, axis=0)
    w = g_ref[pl.ds(row0, PEER_TB), :] * _gelu_exact(s)
    for t in range(PEER_TB):
        w_t = w[t:t + 1, :]
        acc = jnp.zeros((2, SUBLANES, LANES), F32)
        for c in range(PEER_SEL // PEER_KC):
            base = t * PEER_SEL + c * PEER_KC
            a = onehot_ref[c * PEER_KC:(c + 1) * PEER_KC] * w_t[None]
            wsp = jnp.dot(a.reshape(PEER_KC * SUBLANES, LANES), ones,
                          preferred_element_type=F32).reshape(PEER_KC, 1, SUBLANES, LANES)
            v = buf[base:base + PEER_KC, ROW_TILES:2 * ROW_TILES, :].astype(F32)
            acc = acc + jnp.sum(v.reshape(PEER_KC, 2, SUBLANES, LANES) * wsp, axis=0)
        o_ref[row0 + t] = x_ref[row0 + t] + acc.reshape(ROW_TILES, LANES)
        nxt.issue(t, half, PEER_SEL)


def _peer_kernel(ids_cur, ids_nxt, table_ref, h_ref, g_ref, x_ref, ones_ref, onehot_ref, o_ref, *scratch):
    bufs, sems = scratch[:PEER_NBUF], scratch[PEER_NBUF:]
    i = pl.program_id(0)
    cur = [_Prefetch(table_ref, ids_cur, b * PEER_TB, bufs[b], sems[b]) for b in range(PEER_NBUF)]
    nxt = [_Prefetch(table_ref, ids_nxt, b * PEER_TB, bufs[b], sems[b]) for b in range(PEER_AHEAD)]

    @pl.when(i == 0)
    def _():
        for b in range(PEER_AHEAD):
            cur[b].issue_all()

    for b in range(PEER_NBUF):
        cur[b].wait()
        ahead = cur[b + PEER_AHEAD] if b + PEER_AHEAD < PEER_NBUF else nxt[b + PEER_AHEAD - PEER_NBUF]
        _peer_compute(bufs[b], h_ref, g_ref, x_ref, o_ref, b * PEER_TB, ones_ref, onehot_ref, ahead)

    @pl.when(i == pl.num_programs(0) - 1)
    def _():
        for b in range(PEER_AHEAD):
            nxt[b].wait()


def peer_experts(ids, gates, table, h, x):
    t = h.shape[0]
    tb2 = PEER_NBUF * PEER_TB
    nblk = t // tb2
    h3 = h.reshape(t, ROW_TILES, LANES)
    x3 = x.reshape(t, ROW_TILES, LANES)
    ones = jnp.ones((LANES, LANES), F32)
    onehot = jnp.asarray(np.broadcast_to(np.eye(PEER_SEL, LANES, dtype=np.float32)[:, None, :],
                                         (PEER_SEL, SUBLANES, LANES)))
    out = pl.pallas_call(
        _peer_kernel,
        grid=(nblk,),
        in_specs=[
            pl.BlockSpec((tb2, PEER_SEL), lambda i: (i, 0), memory_space=pltpu.SMEM),
            pl.BlockSpec((tb2, PEER_SEL), lambda i: (jnp.minimum(i + 1, nblk - 1), 0), memory_space=pltpu.SMEM),
            pl.BlockSpec(memory_space=pl.ANY),
            pl.BlockSpec((tb2, ROW_TILES, LANES), lambda i: (i, 0, 0)),
            pl.BlockSpec((tb2, PEER_SEL), lambda i: (i, 0)),
            pl.BlockSpec((tb2, ROW_TILES, LANES), lambda i: (i, 0, 0)),
            pl.BlockSpec((LANES, LANES), lambda i: (0, 0)),
            pl.BlockSpec((PEER_SEL, SUBLANES, LANES), lambda i: (0, 0, 0)),
        ],
        out_specs=pl.BlockSpec((tb2, ROW_TILES, LANES), lambda i: (i, 0, 0)),
        out_shape=jax.ShapeDtypeStruct((t, ROW_TILES, LANES), F32),
        scratch_shapes=([pltpu.VMEM((PEER_ROWS, 2 * ROW_TILES, LANES), table.dtype)] * PEER_NBUF
                        + [pltpu.SemaphoreType.DMA] * PEER_NBUF),
        compiler_params=_cparams(("arbitrary",)),
        name="peer_experts",
    )(ids, ids, table, h3, gates, x3, ones, onehot)
    return out.reshape(t, D_MODEL)


def _final_norm_kernel(x_ref, w_ref, o_ref):
    x = x_ref[...]
    ms = jnp.mean(x * x, axis=-1, keepdims=True)
    o_ref[...] = x * lax.rsqrt(ms + NORM_EPS) * w_ref[...]


def final_norm(x, w, *, tm):
    t, d = x.shape
    return pl.pallas_call(
        _final_norm_kernel,
        grid=(t // tm,),
        in_specs=[pl.BlockSpec((tm, d), lambda i: (i, 0)), pl.BlockSpec((1, d), lambda i: (0, 0))],
        out_specs=pl.BlockSpec((tm, d), lambda i: (i, 0)),
        out_shape=jax.ShapeDtypeStruct((t, d), F32),
        compiler_params=_cparams(("arbitrary",)),
        name="final_norm",
    )(x, w.reshape(1, d))


def _split_w_in(w_in):
    w_ssm = w_in[:, :SSM_PART].astype(BF16)
    w_dt = jnp.pad(w_in[:, SSM_PART:SSM_PART + SSM_HEADS], ((0, 0), (0, DT_PAD - SSM_HEADS))).astype(BF16)
    w_ret = w_in[:, SSM_PART + SSM_HEADS:].astype(BF16)
    return w_ssm, w_dt, w_ret


def _mixer_layer(x, positions, norm_mix, w_in, conv_w, conv_b, dt_bias, a_log, d_skip, ssm_norm,
                 w_ssm_out, w_ret_out, b_gate, w_o):
    w_ssm, w_dt, w_ret = _split_w_in(w_in)
    proj_ssm = norm_matmul(x, norm_mix, w_ssm, tm=1024, tn=1024)
    dt_raw = norm_matmul(x, norm_mix, w_dt, tm=1024, tn=DT_PAD)
    proj_ret = norm_matmul(x, norm_mix, w_ret, tm=1024, tn=1024)
    y_ssm = ssd_branch(proj_ssm, dt_raw, conv_w, conv_b, dt_bias, a_log, d_skip, ssm_norm, w_ssm_out.astype(BF16))
    y_ret = retention_branch(proj_ret, positions, w_ret_out.astype(BF16))
    return merge_out_proj(y_ssm, y_ret, proj_ret, b_gate, w_o.astype(BF16), x, tm=256)


def _peer_layer(x, norm_ffn, w_query, sub_keys, expert_u, expert_v):
    q, h = norm_matmul(x, norm_ffn, w_query.astype(BF16), tm=1024, tn=1024, emit_h=True)
    ids, gates = peer_route(q, sub_keys, tt=256)
    table = jnp.concatenate([expert_u, expert_v], axis=1).astype(BF16).reshape(PEER_EXPERTS, 2 * ROW_TILES, LANES)
    return peer_experts(ids, gates, table, h, x)


def kernel(x, positions, norm_mix, w_in, conv_w, conv_b, dt_bias, a_log, d_skip, ssm_norm, w_ssm_out, w_ret_out,
           b_gate, w_o, norm_ffn, w_query, sub_keys, expert_u, expert_v, norm_final):
    b, t, d = x.shape
    depth = norm_mix.shape[0]
    outs = []
    for bi in range(b):
        xb = x[bi]
        pos = positions[bi]
        for l in range(depth):
            xb = _mixer_layer(xb, pos, norm_mix[l], w_in[l], conv_w[l], conv_b[l], dt_bias[l], a_log[l], d_skip[l],
                              ssm_norm[l], w_ssm_out[l], w_ret_out[l], b_gate[l], w_o[l])
            xb = _peer_layer(xb, norm_ffn[l], w_query[l], sub_keys[l], expert_u[l], expert_v[l])
        outs.append(final_norm(xb, norm_final, tm=512))
    return jnp.stack(outs, axis=0)
```

```python
import functools

import jax
import jax.numpy as jnp
import numpy as np
from jax import lax
from jax.experimental import pallas as pl
from jax.experimental.pallas import tpu as pltpu

F32 = jnp.float32
BF16 = jnp.bfloat16

D_MODEL = 2048
CHUNK = 128
SSM_D_INNER = 2 * D_MODEL
SSM_HEAD_DIM = 64
SSM_HEADS = SSM_D_INNER // SSM_HEAD_DIM
SSM_GROUPS = 8
SSM_HPG = SSM_HEADS // SSM_GROUPS
SSM_STATE = 128
SSM_GROUP_WIDTH = SSM_D_INNER // SSM_GROUPS
CONV_WIDTH = 4
RET_HEADS = 8
RET_QK_DIM = D_MODEL // RET_HEADS
RET_V_DIM = 2 * RET_QK_DIM
RET_QK_WIDTH = RET_HEADS * RET_QK_DIM
RET_V_WIDTH = RET_HEADS * RET_V_DIM
ROPE_BASE = 10000.0
PEER_HEADS = 8
PEER_N_KEYS = 128
PEER_EXPERTS = PEER_N_KEYS * PEER_N_KEYS
PEER_TOPK = 16
PEER_QUERY_DIM = 256
PEER_HALF = PEER_QUERY_DIM // 2
PEER_SEL = PEER_HEADS * PEER_TOPK
NORM_EPS = 1e-6

LANES = 128
SUBLANES = 8
VMEM_LIMIT_BYTES = 56 * 1024 * 1024

SSM_NST = SSM_GROUPS * SSM_STATE
OFF_Z = 0
OFF_XS = OFF_Z + SSM_D_INNER
OFF_B = OFF_XS + SSM_D_INNER
OFF_C = OFF_B + SSM_NST
SSM_PART = OFF_C + SSM_NST
DT_PAD = LANES
OFF_RQ = 0
OFF_RK = OFF_RQ + RET_QK_WIDTH
OFF_RV = OFF_RK + RET_QK_WIDTH
OFF_RG = OFF_RV + RET_V_WIDTH
OFF_GATE = OFF_RG + RET_V_WIDTH
RET_PART = OFF_GATE + 2 * D_MODEL

def _cparams(semantics):
    return pltpu.CompilerParams(dimension_semantics=semantics, vmem_limit_bytes=VMEM_LIMIT_BYTES)


def _silu(v):
    return v * (1.0 / (1.0 + jnp.exp(-v)))


def _sigmoid(v):
    return 1.0 / (1.0 + jnp.exp(-v))


def _softplus(v):
    return jnp.maximum(v, 0.0) + jnp.log(1.0 + jnp.exp(-jnp.abs(v)))


def _norm_matmul_kernel(x_ref, nw_ref, w_ref, o_ref, *rest, emit_h):
    if emit_h:
        h_out_ref, h_ref = rest
    else:
        (h_ref,) = rest

    @pl.when(pl.program_id(1) == 0)
    def _():
        x = x_ref[...]
        ms = jnp.mean(x * x, axis=-1, keepdims=True)
        h = x * lax.rsqrt(ms + NORM_EPS) * nw_ref[...]
        h_ref[...] = h.astype(BF16)
        if emit_h:
            h_out_ref[...] = h

    o_ref[...] = jnp.dot(h_ref[...], w_ref[...], preferred_element_type=F32)


def norm_matmul(x, nw, w, *, tm, tn, emit_h=False):
    t, d = x.shape
    n = w.shape[1]
    out_shape = [jax.ShapeDtypeStruct((t, n), F32)]
    out_specs = [pl.BlockSpec((tm, tn), lambda i, j: (i, j))]
    if emit_h:
        out_shape.append(jax.ShapeDtypeStruct((t, d), F32))
        out_specs.append(pl.BlockSpec((tm, d), lambda i, j: (i, 0)))
    res = pl.pallas_call(
        functools.partial(_norm_matmul_kernel, emit_h=emit_h),
        grid=(t // tm, n // tn),
        in_specs=[
            pl.BlockSpec((tm, d), lambda i, j: (i, 0)),
            pl.BlockSpec((1, d), lambda i, j: (0, 0)),
            pl.BlockSpec((d, tn), lambda i, j: (0, j)),
        ],
        out_specs=out_specs,
        out_shape=out_shape,
        scratch_shapes=[pltpu.VMEM((tm, d), BF16)],
        compiler_params=_cparams(("arbitrary", "arbitrary")),
        name="norm_matmul",
    )(x, nw.reshape(1, d), w)
    return res if emit_h else res[0]


def _causal_conv_silu(x, tail_ref, w_ref, b_ref):
    tail = tail_ref[...]
    w = w_ref[...]
    row8 = lax.broadcasted_iota(jnp.int32, (SUBLANES, x.shape[1]), 0)
    acc = x * w[CONV_WIDTH - 1:CONV_WIDTH, :] + b_ref[...]
    for s in range(1, CONV_WIDTH):
        rolled = pltpu.roll(x, s, axis=0)
        head = jnp.where(row8 < s, pltpu.roll(tail, s, axis=0), rolled[:SUBLANES])
        shifted = jnp.concatenate([head, rolled[SUBLANES:]], axis=0)
        acc = acc + shifted * w[CONV_WIDTH - 1 - s:CONV_WIDTH - s, :]
    tail_ref[...] = x[CHUNK - SUBLANES:]
    return _silu(acc)


def _split3(a):
    hi = a.astype(BF16)
    r1 = a - hi.astype(F32)
    mid = r1.astype(BF16)
    lo = (r1 - mid.astype(F32)).astype(BF16)
    return hi, mid, lo


def _dot_right_01(a, m01):
    return sum(jnp.dot(part, m01, preferred_element_type=F32) for part in _split3(a))


def _dot_left_01(m01, a):
    return sum(jnp.dot(m01, part, preferred_element_type=F32) for part in _split3(a))


def _ssd_kernel(z_ref, xs_ref, b_ref, c_ref, dt_ref,
                cw_xs_ref, cb_xs_ref, cw_b_ref, cb_b_ref, cw_c_ref, cb_c_ref,
                dtb_ref, alog_ref, dskip_ref, nw_ref, expand_ref,
                y_ref,
                tail_xs, tail_b, tail_c, state_ref):
    @pl.when(pl.program_id(0) == 0)
    def _():
        tail_xs[...] = jnp.zeros_like(tail_xs)
        tail_b[...] = jnp.zeros_like(tail_b)
        tail_c[...] = jnp.zeros_like(tail_c)
        state_ref[...] = jnp.zeros_like(state_ref)

    xs = _causal_conv_silu(xs_ref[...], tail_xs, cw_xs_ref, cb_xs_ref)
    bm = _causal_conv_silu(b_ref[...], tail_b, cw_b_ref, cb_b_ref)
    cm = _causal_conv_silu(c_ref[...], tail_c, cw_c_ref, cb_c_ref)

    dt = _softplus(dt_ref[...] + dtb_ref[...])
    la = dt * (-jnp.exp(alog_ref[...]))
    row = lax.broadcasted_iota(jnp.int32, (CHUNK, CHUNK), 0)
    col = lax.broadcasted_iota(jnp.int32, (CHUNK, CHUNK), 1)
    causal = row >= col
    cum = _dot_left_01(causal.astype(BF16), la)
    cum_t = cum.T
    total = cum[CHUNK - 1:CHUNK, :]

    expand = expand_ref[...]
    dt_x = _dot_right_01(dt, expand)
    ecum_x = _dot_right_01(jnp.exp(cum), expand)
    edec_x = _dot_right_01(jnp.exp(total - cum), expand)
    xdt = xs * dt_x
    v_dec = (xdt * edec_x).astype(BF16)
    xdt_b = xdt.astype(BF16)
    etot_x = ecum_x[CHUNK - 1:CHUNK, :]

    lane = lax.broadcasted_iota(jnp.int32, (CHUNK, 2 * SSM_HEAD_DIM), 1)
    left = lane < SSM_HEAD_DIM
    y_parts = []
    for g in range(SSM_GROUPS):
        ns = slice(g * SSM_STATE, (g + 1) * SSM_STATE)
        gs = slice(g * SSM_GROUP_WIDTH, (g + 1) * SSM_GROUP_WIDTH)
        c_g = cm[:, ns].astype(BF16)
        b_g = bm[:, ns].astype(BF16)
        scores = lax.dot_general(c_g, b_g, (((1,), (1,)), ((), ())), preferred_element_type=F32)
        prev = state_ref[g]
        y_inter = jnp.dot(c_g, prev.astype(BF16), preferred_element_type=F32) * ecum_x[:, gs]
        intra = []
        for pr in range(SSM_HPG // 2):
            cs = slice(g * SSM_GROUP_WIDTH + pr * 2 * SSM_HEAD_DIM, g * SSM_GROUP_WIDTH + (pr + 1) * 2 * SSM_HEAD_DIM)
            v_pair = xdt_b[:, cs]
            outs = []
            for sub in range(2):
                hd = g * SSM_HPG + pr * 2 + sub
                seg = cum[:, hd:hd + 1] - cum_t[hd:hd + 1, :]
                decay = jnp.exp(jnp.where(causal, seg, -1e30))
                m = (scores * decay).astype(BF16)
                outs.append(jnp.dot(m, v_pair, preferred_element_type=F32))
            intra.append(jnp.where(left, outs[0], outs[1]))
        y_g = jnp.concatenate(intra, axis=1) + y_inter
        st = lax.dot_general(b_g, v_dec[:, gs], (((0,), (0,)), ((), ())), preferred_element_type=F32)
        state_ref[g] = etot_x[:, gs] * prev + st
        y_g = y_g + xs[:, gs] * dskip_ref[:, gs]
        y_g = y_g * _silu(z_ref[:, gs])
        ms = jnp.mean(y_g * y_g, axis=-1, keepdims=True)
        y_parts.append(y_g * lax.rsqrt(ms + NORM_EPS) * nw_ref[:, gs])
    y_ref[...] = jnp.concatenate(y_parts, axis=1).astype(y_ref.dtype)


def ssd_branch(proj, dt_raw, cw, cb, dt_bias, a_log, d_skip, ssm_norm):
    t = proj.shape[0]
    nst = SSM_GROUPS * SSM_STATE
    cw_xs, cw_b, cw_c = cw[:, :SSM_D_INNER], cw[:, SSM_D_INNER:SSM_D_INNER + nst], cw[:, SSM_D_INNER + nst:]
    cb = cb.reshape(1, -1)
    cb_xs, cb_b, cb_c = cb[:, :SSM_D_INNER], cb[:, SSM_D_INNER:SSM_D_INNER + nst], cb[:, SSM_D_INNER + nst:]
    pad = DT_PAD - SSM_HEADS
    dtb = jnp.pad(dt_bias, (0, pad)).reshape(1, DT_PAD)
    alog = jnp.pad(a_log, (0, pad)).reshape(1, DT_PAD)
    dskip = jnp.repeat(d_skip, SSM_HEAD_DIM).reshape(1, SSM_D_INNER)
    expand = (np.arange(DT_PAD)[:, None] == (np.arange(SSM_D_INNER)[None, :] // SSM_HEAD_DIM)).astype(np.float32)

    def whole(a):
        return pl.BlockSpec(a.shape, lambda c: (0,) * a.ndim)

    consts = [cw_xs, cb_xs, cw_b, cb_b, cw_c, cb_c, dtb, alog, dskip, ssm_norm.reshape(1, -1),
              jnp.asarray(expand, dtype=BF16)]
    return pl.pallas_call(
        _ssd_kernel,
        grid=(t // CHUNK,),
        in_specs=[
            pl.BlockSpec((CHUNK, SSM_D_INNER), lambda c: (c, OFF_Z // SSM_D_INNER)),
            pl.BlockSpec((CHUNK, SSM_D_INNER), lambda c: (c, OFF_XS // SSM_D_INNER)),
            pl.BlockSpec((CHUNK, nst), lambda c: (c, OFF_B // nst)),
            pl.BlockSpec((CHUNK, nst), lambda c: (c, OFF_C // nst)),
            pl.BlockSpec((CHUNK, DT_PAD), lambda c: (c, 0)),
        ] + [whole(a) for a in consts],
        out_specs=pl.BlockSpec((CHUNK, SSM_D_INNER), lambda c: (c, 0)),
        out_shape=jax.ShapeDtypeStruct((t, SSM_D_INNER), BF16),
        scratch_shapes=[
            pltpu.VMEM((SUBLANES, SSM_D_INNER), F32),
            pltpu.VMEM((SUBLANES, nst), F32),
            pltpu.VMEM((SUBLANES, nst), F32),
            pltpu.VMEM((SSM_GROUPS, SSM_STATE, SSM_GROUP_WIDTH), F32),
        ],
        compiler_params=_cparams(("arbitrary",)),
        name="ssd_branch",
    )(proj, proj, proj, proj, dt_raw, *consts)


def _ret_log_gamma(h):
    return float(np.log1p(-np.exp2(-5.0 - h)))


def _rotary(x, cos, sin):
    half = RET_QK_DIM // 2
    x1, x2 = x[:, :half], x[:, half:]
    return jnp.concatenate([x1 * cos - x2 * sin, x2 * cos + x1 * sin], axis=1)


def _ret_kernel(q_ref, k_ref, v_ref, g_ref, pos_ref, freq_ref, o_ref, state_ref):
    @pl.when(pl.program_id(0) == 0)
    def _():
        state_ref[...] = jnp.zeros_like(state_ref)

    ang = pos_ref[...].astype(F32) * freq_ref[...]
    cos, sin = jnp.cos(ang), jnp.sin(ang)
    row = lax.broadcasted_iota(jnp.int32, (CHUNK, CHUNK), 0)
    col = lax.broadcasted_iota(jnp.int32, (CHUNK, CHUNK), 1)
    causal = row >= col
    rel = (row - col).astype(F32)
    r1 = lax.broadcasted_iota(jnp.int32, (CHUNK, 1), 0).astype(F32)
    outs = []
    for h in range(RET_HEADS):
        lg = _ret_log_gamma(h)
        qs = slice(h * RET_QK_DIM, (h + 1) * RET_QK_DIM)
        vs = slice(h * RET_V_DIM, (h + 1) * RET_V_DIM)
        q = _rotary(q_ref[:, qs], cos, sin).astype(BF16)
        k = (_rotary(k_ref[:, qs], cos, sin) * (RET_QK_DIM ** -0.5)).astype(BF16)
        v = v_ref[:, vs]
        scores = lax.dot_general(q, k, (((1,), (1,)), ((), ())), preferred_element_type=F32)
        decay = jnp.exp(jnp.where(causal, rel * lg, -1e30))
        o = jnp.dot((scores * decay).astype(BF16), v.astype(BF16), preferred_element_type=F32)
        prev = state_ref[h]
        o = o + jnp.dot(q, prev.astype(BF16), preferred_element_type=F32) * jnp.exp((r1 + 1.0) * lg)
        v_dec = (v * jnp.exp((CHUNK - 1.0 - r1) * lg)).astype(BF16)
        st = lax.dot_general(k, v_dec, (((0,), (0,)), ((), ())), preferred_element_type=F32)
        state_ref[h] = float(np.exp(CHUNK * lg)) * prev + st
        ms = jnp.mean(o * o, axis=-1, keepdims=True)
        outs.append(o * lax.rsqrt(ms + NORM_EPS) * _silu(g_ref[:, vs]))
    o_ref[...] = jnp.concatenate(outs, axis=1).astype(o_ref.dtype)


def retention_branch(proj, positions):
    t = proj.shape[0]
    half = RET_QK_DIM // 2
    inv_freq = (1.0 / (ROPE_BASE ** (np.arange(half, dtype=np.float32) / half))).astype(np.float32)
    return pl.pallas_call(
        _ret_kernel,
        grid=(t // CHUNK,),
        in_specs=[
            pl.BlockSpec((CHUNK, RET_QK_WIDTH), lambda c: (c, OFF_RQ // RET_QK_WIDTH)),
            pl.BlockSpec((CHUNK, RET_QK_WIDTH), lambda c: (c, OFF_RK // RET_QK_WIDTH)),
            pl.BlockSpec((CHUNK, RET_V_WIDTH), lambda c: (c, OFF_RV // RET_V_WIDTH)),
            pl.BlockSpec((CHUNK, RET_V_WIDTH), lambda c: (c, OFF_RG // RET_V_WIDTH)),
            pl.BlockSpec((CHUNK, 1), lambda c: (c, 0)),
            pl.BlockSpec((1, half), lambda c: (0, 0)),
        ],
        out_specs=pl.BlockSpec((CHUNK, RET_V_WIDTH), lambda c: (c, 0)),
        out_shape=jax.ShapeDtypeStruct((t, RET_V_WIDTH), BF16),
        scratch_shapes=[pltpu.VMEM((RET_HEADS, RET_QK_DIM, RET_V_DIM), F32)],
        compiler_params=_cparams(("arbitrary",)),
        name="retention_branch",
    )(proj, proj, proj, proj, positions.reshape(t, 1), jnp.asarray(inv_freq).reshape(1, half))


def _merge_kernel(y_ref, o_ref, ws_ref, wr_ref, gs_ref, gr_ref, bs_ref, br_ref, m_ref):
    y_ssm = jnp.dot(y_ref[...], ws_ref[...], preferred_element_type=F32)
    y_ret = jnp.dot(o_ref[...], wr_ref[...], preferred_element_type=F32)
    m = _sigmoid(gs_ref[...] + bs_ref[...]) * y_ssm + _sigmoid(gr_ref[...] + br_ref[...]) * y_ret
    m_ref[...] = m.astype(m_ref.dtype)


def gated_merge(y, o, w_ssm_out, w_ret_out, proj, b_gate, *, tm, tn):
    t = y.shape[0]
    nj = D_MODEL // tn
    bg = b_gate.reshape(1, 2 * D_MODEL)
    return pl.pallas_call(
        _merge_kernel,
        grid=(t // tm, nj),
        in_specs=[
            pl.BlockSpec((tm, SSM_D_INNER), lambda i, j: (i, 0)),
            pl.BlockSpec((tm, RET_V_WIDTH), lambda i, j: (i, 0)),
            pl.BlockSpec((SSM_D_INNER, tn), lambda i, j: (0, j)),
            pl.BlockSpec((RET_V_WIDTH, tn), lambda i, j: (0, j)),
            pl.BlockSpec((tm, tn), lambda i, j: (i, OFF_GATE // tn + j)),
            pl.BlockSpec((tm, tn), lambda i, j: (i, OFF_GATE // tn + nj + j)),
            pl.BlockSpec((1, tn), lambda i, j: (0, j)),
            pl.BlockSpec((1, tn), lambda i, j: (0, nj + j)),
        ],
        out_specs=pl.BlockSpec((tm, tn), lambda i, j: (i, j)),
        out_shape=jax.ShapeDtypeStruct((t, D_MODEL), BF16),
        compiler_params=_cparams(("arbitrary", "arbitrary")),
        name="gated_merge",
    )(y, o, w_ssm_out, w_ret_out, proj, proj, bg, bg)


def _out_proj_kernel(m_ref, w_ref, x_ref, o_ref):
    o_ref[...] = x_ref[...] + jnp.dot(m_ref[...], w_ref[...], preferred_element_type=F32)


def out_proj_residual(m, w_o, x, *, tm, tn):
    t = m.shape[0]
    return pl.pallas_call(
        _out_proj_kernel,
        grid=(t // tm, D_MODEL // tn),
        in_specs=[
            pl.BlockSpec((tm, D_MODEL), lambda i, j: (i, 0)),
            pl.BlockSpec((D_MODEL, tn), lambda i, j: (0, j)),
            pl.BlockSpec((tm, tn), lambda i, j: (i, j)),
        ],
        out_specs=pl.BlockSpec((tm, tn), lambda i, j: (i, j)),
        out_shape=jax.ShapeDtypeStruct((t, D_MODEL), F32),
        compiler_params=_cparams(("arbitrary", "arbitrary")),
        name="out_proj_residual",
    )(m, w_o, x)


def _top_k_rows(vals, k, payload=None):
    n, w = vals.shape
    row = lax.broadcasted_iota(jnp.int32, (n, w), 0).astype(F32)
    out_row = lax.broadcasted_iota(jnp.int32, (k, w), 0)
    top_v = jnp.zeros((k, w), F32)
    top_i = jnp.zeros((k, w), F32)
    for i in range(k):
        m = jnp.max(vals, axis=0, keepdims=True)
        sel = jnp.min(jnp.where(vals == m, row, float(n)), axis=0, keepdims=True)
        hit = row == sel
        picked = sel if payload is None else jnp.max(jnp.where(hit, payload, -1.0), axis=0, keepdims=True)
        top_v = jnp.where(out_row == i, m, top_v)
        top_i = jnp.where(out_row == i, picked, top_i)
        vals = jnp.where(hit, -jnp.inf, vals)
    return top_v, top_i


_CAND_PAIRS = [(a, b) for a in range(PEER_TOPK) for b in range(PEER_TOPK) if (a + 1) * (b + 1) <= PEER_TOPK]
_CAND_ROWS = -(-len(_CAND_PAIRS) // 16) * 16


def _candidate_pickers():
    pick = np.zeros((2, _CAND_ROWS, PEER_TOPK), np.float32)
    for r, (a, b) in enumerate(_CAND_PAIRS):
        pick[0, r, a] = 1.0
        pick[1, r, b] = 1.0
    return jnp.asarray(pick, dtype=BF16)


def _route_kernel(q_ref, keys_ref, pick_ref, ids_ref, gates_ref):
    q = q_ref[...]
    tt = q.shape[0]
    pick_a, pick_b = pick_ref[0], pick_ref[1]
    pad = jnp.where(lax.broadcasted_iota(jnp.int32, (_CAND_ROWS, tt), 0) < len(_CAND_PAIRS), 0.0, -jnp.inf)
    ids, gates = [], []
    for h in range(PEER_HEADS):
        q1 = q[:, h * PEER_QUERY_DIM:h * PEER_QUERY_DIM + PEER_HALF]
        q2 = q[:, h * PEER_QUERY_DIM + PEER_HALF:(h + 1) * PEER_QUERY_DIM]
        s1 = lax.dot_general(keys_ref[h, 0], q1, (((1,), (1,)), ((), ())), preferred_element_type=F32)
        s2 = lax.dot_general(keys_ref[h, 1], q2, (((1,), (1,)), ((), ())), preferred_element_type=F32)
        v1, i1 = _top_k_rows(s1, PEER_TOPK)
        v2, i2 = _top_k_rows(s2, PEER_TOPK)
        cand = _dot_left_01(pick_a, v1) + _dot_left_01(pick_b, v2) + pad
        cand_id = (jnp.dot(pick_a, i1.astype(BF16), preferred_element_type=F32) * PEER_N_KEYS
                   + jnp.dot(pick_b, i2.astype(BF16), preferred_element_type=F32))
        top_s, top_id = _top_k_rows(cand, PEER_TOPK, payload=cand_id)
        ids.append(top_id)
        ex = jnp.exp(top_s - top_s[0:1, :])
        gates.append(ex / jnp.sum(ex, axis=0, keepdims=True))
    ids_ref[...] = jnp.concatenate(ids, axis=0).T.astype(jnp.int32)
    gates_ref[...] = jnp.concatenate(gates, axis=0).T


def peer_route(q, sub_keys, *, tt):
    t = q.shape[0]
    pick = _candidate_pickers()
    return pl.pallas_call(
        _route_kernel,
        grid=(t // tt,),
        in_specs=[
            pl.BlockSpec((tt, q.shape[1]), lambda i: (i, 0)),
            pl.BlockSpec(sub_keys.shape, lambda i: (0, 0, 0, 0)),
            pl.BlockSpec(pick.shape, lambda i: (0, 0, 0)),
        ],
        out_specs=[
            pl.BlockSpec((tt, PEER_SEL), lambda i: (i, 0)),
            pl.BlockSpec((tt, PEER_SEL), lambda i: (i, 0)),
        ],
        out_shape=[
            jax.ShapeDtypeStruct((t, PEER_SEL), jnp.int32),
            jax.ShapeDtypeStruct((t, PEER_SEL), F32),
        ],
        compiler_params=_cparams(("arbitrary",)),
        name="peer_route",
    )(q, sub_keys, pick)


ROW_TILES = D_MODEL // LANES
PEER_TB = 8
PEER_ROWS = PEER_TB * PEER_SEL
PEER_KC = 16
PEER_NBUF = 4
PEER_AHEAD = 2


def _erf(x):
    x = jnp.clip(x, -4.0, 4.0)
    x2 = x * x
    p = -2.72614225801306e-10
    for c in (2.77068142495902e-08, -2.10102402082508e-06, -5.69250639462346e-05,
              -7.34990630326855e-04, -2.95459980854025e-03, -1.60960333262415e-02):
        p = p * x2 + c
    q = -1.45660718464996e-05
    for c in (-2.13374055278905e-04, -1.68282697438203e-03, -7.37332916720468e-03, -1.42647390514189e-02):
        q = q * x2 + c
    return x * p / q


def _gelu_exact(x):
    return 0.5 * x * (1.0 + _erf(x * float(1.0 / np.sqrt(2.0))))


class _Prefetch:
    def __init__(self, table_ref, ids_ref, ids_row0, buf, sem):
        self.table_ref, self.ids_ref, self.ids_row0, self.buf, self.sem = table_ref, ids_ref, ids_row0, buf, sem

    def issue(self, t, k0, k1):
        for k in range(k0, k1):
            e = self.ids_ref[self.ids_row0 + t, k]
            pltpu.make_async_copy(self.table_ref.at[e], self.buf.at[t * PEER_SEL + k], self.sem).start(priority=k % 2)

    def issue_all(self):
        def body(t, carry):
            for k in range(PEER_SEL):
                e = self.ids_ref[self.ids_row0 + t, k]
                pltpu.make_async_copy(self.table_ref.at[e], self.buf.at[t * PEER_SEL + k], self.sem).start(priority=k % 2)
            return carry
        lax.fori_loop(0, PEER_TB, body, 0)

    def wait(self):
        pltpu.make_async_copy(self.table_ref.at[pl.ds(0, PEER_ROWS)], self.buf, self.sem).wait()


def _peer_compute(buf, h_ref, g_ref, x_ref, o_ref, row0, ones_ref, onehot_ref, nxt):
    ones = ones_ref[...]
    half = PEER_SEL // 2
    s_rows = []
    for t in range(PEER_TB):
        h = h_ref[row0 + t]
        z = jnp.zeros((SUBLANES, LANES), F32)
        for c in range(PEER_SEL // PEER_KC):
            base = t * PEER_SEL + c * PEER_KC
            u = buf[base:base + PEER_KC, 0:ROW_TILES, :].astype(F32)
            p = u * h[None]
            p8 = p[:, :SUBLANES] + p[:, SUBLANES:]
            r = jnp.dot(p8.reshape(PEER_KC * SUBLANES, LANES), ones, preferred_element_type=F32)
            r = r.reshape(PEER_KC, SUBLANES, LANES)
            z = z + jnp.sum(r * onehot_ref[c * PEER_KC:(c + 1) * PEER_KC], axis=0)
        s_rows.append(jnp.sum(z, axis=0, keepdims=True))
        nxt.issue(t, 0, half)
    s = jnp.concatenate(s_rows, axis=0)
    w = g_ref[pl.ds(row0, PEER_TB), :] * _gelu_exact(s)
    for t in range(PEER_TB):
        w_t = w[t:t + 1, :]
        acc = jnp.zeros((2, SUBLANES, LANES), F32)
        for c in range(PEER_SEL // PEER_KC):
            base = t * PEER_SEL + c * PEER_KC
            a = onehot_ref[c * PEER_KC:(c + 1) * PEER_KC] * w_t[None]
            wsp = jnp.dot(a.reshape(PEER_KC * SUBLANES, LANES), ones,
                          preferred_element_type=F32).reshape(PEER_KC, 1, SUBLANES, LANES)
            v = buf[base:base + PEER_KC, ROW_TILES:2 * ROW_TILES, :].astype(F32)
            acc = acc + jnp.sum(v.reshape(PEER_KC, 2, SUBLANES, LANES) * wsp, axis=0)
        o_ref[row0 + t] = x_ref[row0 + t] + acc.reshape(ROW_TILES, LANES)
        nxt.issue(t, half, PEER_SEL)


def _peer_kernel(ids_cur, ids_nxt, table_ref, h_ref, g_ref, x_ref, ones_ref, onehot_ref, o_ref, *scratch):
    bufs, sems = scratch[:PEER_NBUF], scratch[PEER_NBUF:]
    i = pl.program_id(0)
    cur = [_Prefetch(table_ref, ids_cur, b * PEER_TB, bufs[b], sems[b]) for b in range(PEER_NBUF)]
    nxt = [_Prefetch(table_ref, ids_nxt, b * PEER_TB, bufs[b], sems[b]) for b in range(PEER_AHEAD)]

    @pl.when(i == 0)
    def _():
        for b in range(PEER_AHEAD):
            cur[b].issue_all()

    for b in range(PEER_NBUF):
        cur[b].wait()
        ahead = cur[b + PEER_AHEAD] if b + PEER_AHEAD < PEER_NBUF else nxt[b + PEER_AHEAD - PEER_NBUF]
        _peer_compute(bufs[b], h_ref, g_ref, x_ref, o_ref, b * PEER_TB, ones_ref, onehot_ref, ahead)

    @pl.when(i == pl.num_programs(0) - 1)
    def _():
        for b in range(PEER_AHEAD):
            nxt[b].wait()


def _table_kernel(u_ref, v_ref, o_ref):
    r = u_ref.shape[0]
    o_ref[:, 0:ROW_TILES, :] = u_ref[...].reshape(r, ROW_TILES, LANES).astype(BF16)
    o_ref[:, ROW_TILES:2 * ROW_TILES, :] = v_ref[...].reshape(r, ROW_TILES, LANES).astype(BF16)


def expert_table(expert_u, expert_v, *, rows):
    e, d = expert_u.shape
    return pl.pallas_call(
        _table_kernel,
        grid=(e // rows,),
        in_specs=[pl.BlockSpec((rows, d), lambda i: (i, 0)), pl.BlockSpec((rows, d), lambda i: (i, 0))],
        out_specs=pl.BlockSpec((rows, 2 * ROW_TILES, LANES), lambda i: (i, 0, 0)),
        out_shape=jax.ShapeDtypeStruct((e, 2 * ROW_TILES, LANES), BF16),
        compiler_params=_cparams(("arbitrary",)),
        name="expert_table",
    )(expert_u, expert_v)


def peer_experts(ids, gates, table, h, x):
    t = h.shape[0]
    tb2 = PEER_NBUF * PEER_TB
    nblk = t // tb2
    h3 = h.reshape(t, ROW_TILES, LANES)
    x3 = x.reshape(t, ROW_TILES, LANES)
    ones = jnp.ones((LANES, LANES), F32)
    onehot = jnp.asarray(np.broadcast_to(np.eye(PEER_SEL, LANES, dtype=np.float32)[:, None, :],
                                         (PEER_SEL, SUBLANES, LANES)))
    out = pl.pallas_call(
        _peer_kernel,
        grid=(nblk,),
        in_specs=[
            pl.BlockSpec((tb2, PEER_SEL), lambda i: (i, 0), memory_space=pltpu.SMEM),
            pl.BlockSpec((tb2, PEER_SEL), lambda i: (jnp.minimum(i + 1, nblk - 1), 0), memory_space=pltpu.SMEM),
            pl.BlockSpec(memory_space=pl.ANY),
            pl.BlockSpec((tb2, ROW_TILES, LANES), lambda i: (i, 0, 0)),
            pl.BlockSpec((tb2, PEER_SEL), lambda i: (i, 0)),
            pl.BlockSpec((tb2, ROW_TILES, LANES), lambda i: (i, 0, 0)),
            pl.BlockSpec((LANES, LANES), lambda i: (0, 0)),
            pl.BlockSpec((PEER_SEL, SUBLANES, LANES), lambda i: (0, 0, 0)),
        ],
        out_specs=pl.BlockSpec((tb2, ROW_TILES, LANES), lambda i: (i, 0, 0)),
        out_shape=jax.ShapeDtypeStruct((t, ROW_TILES, LANES), F32),
        scratch_shapes=([pltpu.VMEM((PEER_ROWS, 2 * ROW_TILES, LANES), table.dtype)] * PEER_NBUF
                        + [pltpu.SemaphoreType.DMA] * PEER_NBUF),
        compiler_params=_cparams(("arbitrary",)),
        name="peer_experts",
    )(ids, ids, table, h3, gates, x3, ones, onehot)
    return out.reshape(t, D_MODEL)


def _final_norm_kernel(x_ref, w_ref, o_ref):
    x = x_ref[...]
    ms = jnp.mean(x * x, axis=-1, keepdims=True)
    o_ref[...] = x * lax.rsqrt(ms + NORM_EPS) * w_ref[...]


def final_norm(x, w, *, tm):
    t, d = x.shape
    return pl.pallas_call(
        _final_norm_kernel,
        grid=(t // tm,),
        in_specs=[pl.BlockSpec((tm, d), lambda i: (i, 0)), pl.BlockSpec((1, d), lambda i: (0, 0))],
        out_specs=pl.BlockSpec((tm, d), lambda i: (i, 0)),
        out_shape=jax.ShapeDtypeStruct((t, d), F32),
        compiler_params=_cparams(("arbitrary",)),
        name="final_norm",
    )(x, w.reshape(1, d))


def _split_w_in(w_in):
    w_ssm = w_in[:, :SSM_PART].astype(BF16)
    w_dt = jnp.pad(w_in[:, SSM_PART:SSM_PART + SSM_HEADS], ((0, 0), (0, DT_PAD - SSM_HEADS))).astype(BF16)
    w_ret = w_in[:, SSM_PART + SSM_HEADS:].astype(BF16)
    return w_ssm, w_dt, w_ret


def _mixer_layer(x, positions, norm_mix, w_in, conv_w, conv_b, dt_bias, a_log, d_skip, ssm_norm,
                 w_ssm_out, w_ret_out, b_gate, w_o):
    w_ssm, w_dt, w_ret = _split_w_in(w_in)
    proj_ssm = norm_matmul(x, norm_mix, w_ssm, tm=1024, tn=1024)
    dt_raw = norm_matmul(x, norm_mix, w_dt, tm=1024, tn=DT_PAD)
    proj_ret = norm_matmul(x, norm_mix, w_ret, tm=1024, tn=1024)
    y = ssd_branch(proj_ssm, dt_raw, conv_w, conv_b, dt_bias, a_log, d_skip, ssm_norm)
    o = retention_branch(proj_ret, positions)
    m = gated_merge(y, o, w_ssm_out.astype(BF16), w_ret_out.astype(BF16), proj_ret, b_gate, tm=512, tn=512)
    return out_proj_residual(m, w_o.astype(BF16), x, tm=1024, tn=1024)


def _peer_layer(x, norm_ffn, w_query, sub_keys, expert_u, expert_v):
    q, h = norm_matmul(x, norm_ffn, w_query.astype(BF16), tm=1024, tn=1024, emit_h=True)
    ids, gates = peer_route(q, sub_keys, tt=256)
    table = expert_table(expert_u, expert_v, rows=256)
    return peer_experts(ids, gates, table, h, x)


def kernel(x, positions, norm_mix, w_in, conv_w, conv_b, dt_bias, a_log, d_skip, ssm_norm, w_ssm_out, w_ret_out,
           b_gate, w_o, norm_ffn, w_query, sub_keys, expert_u, expert_v, norm_final):
    b, t, d = x.shape
    depth = norm_mix.shape[0]
    outs = []
    for bi in range(b):
        xb = x[bi]
        pos = positions[bi]
        for l in range(depth):
            xb = _mixer_layer(xb, pos, norm_mix[l], w_in[l], conv_w[l], conv_b[l], dt_bias[l], a_log[l], d_skip[l],
                              ssm_norm[l], w_ssm_out[l], w_ret_out[l], b_gate[l], w_o[l])
            xb = _peer_layer(xb, norm_ffn[l], w_query[l], sub_keys[l], expert_u[l], expert_v[l])
        outs.append(final_norm(xb, norm_final, tm=512))
    return jnp.stack(outs, axis=0)
```
